```python
import jax, jax.numpy as jnp
from jax import lax
import numpy as np

D_MODEL = 1024
BATCH = 16
SEQ = 256
DEPTH = 1
DEC_BATCH = 8
DEC_SEQ = 2048
PAST_LEN = 512

GRID_W = 64
EPS = 1e-6
MLA_HEADS = 8
MLA_NOPE = 64
MLA_ROPE = 32
MLA_V = 64
Q_LORA = 256
KV_LORA = 128
ROPE_BASE = 10000.0
ATTN_BLOCK = 128
RET_HEADS = 4
RET_DK = 128
RET_DV = 128
RET_CHUNK = 64
MIX_A = MLA_HEADS * MLA_V
MIX_B = RET_HEADS * RET_DV
IN_SIZES = (Q_LORA, KV_LORA, MLA_ROPE, RET_HEADS * RET_DK, RET_HEADS * RET_DK, MIX_B, MIX_B, D_MODEL, D_MODEL)
D_IN = Q_LORA + KV_LORA + MLA_ROPE + 2 * RET_HEADS * RET_DK + 2 * MIX_B + 2 * D_MODEL
PEER_HEADS = 8
PEER_DQ = 256
N_KEYS = 128
N_EXPERTS = N_KEYS * N_KEYS
PEER_TOPK = 16
PEER_BLOCK = 128

kernel_name = 'hybrid_mla_retention_peer_diffusion_step'


def _split(a, sizes):
    out = []
    start = 0
    for s in sizes:
        out.append(a[..., start:start + s])
        start += s
    return out


def _rmsnorm(x, g):
    xf = x.astype(jnp.float32)
    xf = xf * lax.rsqrt(jnp.mean(xf * xf, axis=-1, keepdims=True) + EPS)
    return xf.astype(x.dtype) * g


def _modulate(h, shift, scale):
    return h * (1 + scale) + shift


def _rope_2d(x):
    T = x.shape[1]
    rows = T // GRID_W
    row = jnp.repeat(jnp.arange(rows, dtype=jnp.float32), GRID_W)
    col = jnp.tile(jnp.arange(GRID_W, dtype=jnp.float32), rows)
    nf = MLA_ROPE // 4
    freqs = jnp.power(ROPE_BASE, -jnp.arange(nf, dtype=jnp.float32) / nf)
    ang = jnp.concatenate([row[:, None] * freqs, col[:, None] * freqs], axis=-1)
    ang = ang.reshape((T,) + (1,) * (x.ndim - 3) + (MLA_ROPE // 2,))
    cos = jnp.cos(ang).astype(x.dtype)
    sin = jnp.sin(ang).astype(x.dtype)
    x1, x2 = x[..., :MLA_ROPE // 2], x[..., MLA_ROPE // 2:]
    return jnp.concatenate([x1 * cos - x2 * sin, x1 * sin + x2 * cos], axis=-1)


def _project(h, w_in, q_norm, kv_norm, w_uq):
    B, T, _ = h.shape
    cq, ckv, kr, rq, rk, rv, rg, ga, gb = _split(h @ w_in, IN_SIZES)
    q = (_rmsnorm(cq, q_norm) @ w_uq).reshape(B, T, MLA_HEADS, MLA_NOPE + MLA_ROPE)
    qn, qr = q[..., :MLA_NOPE], q[..., MLA_NOPE:]
    ckv = _rmsnorm(ckv, kv_norm)
    rq = rq.reshape(B, T, RET_HEADS, RET_DK)
    rk = rk.reshape(B, T, RET_HEADS, RET_DK) * (RET_DK ** -0.5)
    rv = rv.reshape(B, T, RET_HEADS, RET_DV)
    return qn, qr, ckv, kr, rq, rk, rv, rg, ga, gb


def _expand_kv(ckv, w_ukv):
    B, T, _ = ckv.shape
    kv = (ckv @ w_ukv).reshape(B, T, MLA_HEADS, MLA_NOPE + MLA_V)
    return kv[..., :MLA_NOPE], kv[..., MLA_NOPE:]


def _mla_attend(q_nope, q_rope, k_nope, k_rope, v):
    B, Tq, H, _ = q_nope.shape
    nb = Tq // ATTN_BLOCK
    scale = (MLA_NOPE + MLA_ROPE) ** -0.5

    def block(qs):
        qn, qr = qs
        s = jnp.einsum('bqhd,bkhd->bhqk', qn, k_nope) + jnp.einsum('bqhr,bkr->bhqk', qr, k_rope)
        p = jax.nn.softmax(s.astype(jnp.float32) * scale, axis=-1).astype(v.dtype)
        return jnp.einsum('bhqk,bkhd->bqhd', p, v)

    qb = lambda a: jnp.moveaxis(a.reshape(B, nb, ATTN_BLOCK, H, a.shape[-1]), 1, 0)
    out = lax.map(block, (qb(q_nope), qb(q_rope)))
    return jnp.moveaxis(out, 0, 1).reshape(B, Tq, H * MLA_V)


def _retention_dir(q, k, v, decay_logit, s0):
    B, T, H, dk = q.shape
    dv = v.shape[-1]
    C = RET_CHUNK
    n = T // C
    lg = jax.nn.log_sigmoid(decay_logit.astype(jnp.float32))[:, None]
    idx = jnp.arange(C, dtype=jnp.float32)
    diff = idx[:, None] - idx[None, :]
    dmat = jnp.where(diff >= 0, jnp.exp(lg[:, :, None] * jnp.maximum(diff, 0.0)), 0.0)
    q_dec = jnp.exp(lg * (idx + 1.0)).T
    k_dec = jnp.exp(lg * (C - 1.0 - idx)).T
    c_dec = jnp.exp(lg[:, 0] * C)
    chunks = lambda a: jnp.moveaxis(a.astype(jnp.float32).reshape(B, n, C, H, a.shape[-1]), 1, 0)

    def step(S, inp):
        qc, kc, vc = inp
        sc = jnp.einsum('bihd,bjhd->bhij', qc, kc) * dmat
        intra = jnp.einsum('bhij,bjhe->bihe', sc, vc)
        cross = jnp.einsum('bihd,bhde->bihe', qc * q_dec[None, :, :, None], S)
        S_new = S * c_dec[None, :, None, None] + jnp.einsum('bjhd,bjhe->bhde', kc * k_dec[None, :, :, None], vc)
        return S_new, intra + cross

    S_fin, out = lax.scan(step, s0.astype(jnp.float32), (chunks(q), chunks(k), chunks(v)))
    out = jnp.moveaxis(out, 0, 1).reshape(B, T, H, dv)
    return out, S_fin.astype(s0.dtype)


def _retention(rq, rk, rv, rg, logit_f, logit_b, s0_f, s0_b, gn):
    y_f, s_f = _retention_dir(rq, rk, rv, logit_f, s0_f)
    y_b, s_b = _retention_dir(jnp.flip(rq, 1), jnp.flip(rk, 1), jnp.flip(rv, 1), logit_b, s0_b)
    y = y_f + jnp.flip(y_b, 1)
    mu = jnp.mean(y, axis=-1, keepdims=True)
    var = jnp.mean(jnp.square(y - mu), axis=-1, keepdims=True)
    y = ((y - mu) * lax.rsqrt(var + EPS)).astype(rv.dtype)
    B, T = y.shape[0], y.shape[1]
    y = y.reshape(B, T, MIX_B) * gn
    return jax.nn.silu(rg) * y, s_f, s_b


def _merge(attn, ret, ga, gb, w_up_a, w_up_b, w_o):
    return (jax.nn.sigmoid(ga) * (attn @ w_up_a) + jax.nn.sigmoid(gb) * (ret @ w_up_b)) @ w_o


def _peer(h, wq, keys1, keys2, u_tab, v_tab):
    B, T, D = h.shape
    xb_all = h.reshape(-1, PEER_BLOCK, D)

    def block(xb):
        q = (xb @ wq).reshape(PEER_BLOCK, PEER_HEADS, PEER_DQ)
        q1, q2 = q[..., :PEER_DQ // 2], q[..., PEER_DQ // 2:]
        s1 = jnp.einsum('nhd,hkd->nhk', q1, keys1).astype(jnp.float32)
        s2 = jnp.einsum('nhd,hkd->nhk', q2, keys2).astype(jnp.float32)
        v1, i1 = lax.top_k(s1, PEER_TOPK)
        v2, i2 = lax.top_k(s2, PEER_TOPK)
        cand = (v1[..., :, None] + v2[..., None, :]).reshape(PEER_BLOCK, PEER_HEADS, PEER_TOPK * PEER_TOPK)
        vals, ci = lax.top_k(cand, PEER_TOPK)
        e = jnp.take_along_axis(i1, ci // PEER_TOPK, axis=-1) * N_KEYS + jnp.take_along_axis(i2, ci % PEER_TOPK, axis=-1)
        g = jax.nn.softmax(vals, axis=-1).astype(xb.dtype)
        a = jax.nn.gelu(jnp.einsum('nd,nhkd->nhk', xb, u_tab[e]))
        return jnp.einsum('nhk,nhkd->nd', g * a, v_tab[e])

    return lax.map(block, xb_all).reshape(B, T, D)


def setup_inputs(seed: int = 0) -> dict:
    key = jax.random.key(seed)
    ks = iter(jax.random.split(key, 40))
    nrm = lambda shape, s: s * jax.random.normal(next(ks), shape, jnp.float32)
    ones = lambda shape: 1.0 + nrm(shape, 0.01)
    base_logit = jnp.log(jnp.exp2(5.0 + jnp.arange(RET_HEADS, dtype=jnp.float32)) - 1.0)
    return {
        'x_prompt': nrm((BATCH, SEQ, D_MODEL), 1.0),
        'x_sample': nrm((DEC_BATCH, DEC_SEQ, D_MODEL), 1.0),
        'c': nrm((DEC_BATCH, D_MODEL), 1.0),
        'cache_ckv': nrm((DEC_BATCH, DEPTH, PAST_LEN, KV_LORA), 1.0),
        'cache_krope': nrm((DEC_BATCH, DEPTH, PAST_LEN, MLA_ROPE), 1.0),
        'state_ret_fwd': nrm((DEC_BATCH, DEPTH, RET_HEADS, RET_DK, RET_DV), 0.5),
        'state_ret_bwd': nrm((DEC_BATCH, DEPTH, RET_HEADS, RET_DK, RET_DV), 0.5),
        'c_ctx': nrm((D_MODEL,), 1.0),
        'w_mod': nrm((DEPTH, D_MODEL, 6 * D_MODEL), 0.5 * D_MODEL ** -0.5),
        'b_mod': nrm((DEPTH, 6 * D_MODEL), 0.01),
        'norm_mix': ones((DEPTH, D_MODEL)),
        'norm_ffn': ones((DEPTH, D_MODEL)),
        'norm_final': ones((D_MODEL,)),
        'w_in': nrm((DEPTH, D_MODEL, D_IN), D_MODEL ** -0.5),
        'q_norm': ones((DEPTH, Q_LORA)),
        'kv_norm': ones((DEPTH, KV_LORA)),
        'w_uq': nrm((DEPTH, Q_LORA, MLA_HEADS * (MLA_NOPE + MLA_ROPE)), Q_LORA ** -0.5),
        'w_ukv': nrm((DEPTH, KV_LORA, MLA_HEADS * (MLA_NOPE + MLA_V)), KV_LORA ** -0.5),
        'ret_logit_fwd': base_logit[None, :] + nrm((DEPTH, RET_HEADS), 0.01),
        'ret_logit_bwd': base_logit[None, :] + nrm((DEPTH, RET_HEADS), 0.01),
        'ret_gn': ones((DEPTH, MIX_B)),
        'w_up_a': nrm((DEPTH, MIX_A, D_MODEL), MIX_A ** -0.5),
        'w_up_b': nrm((DEPTH, MIX_B, D_MODEL), MIX_B ** -0.5),
        'w_o': nrm((DEPTH, D_MODEL, D_MODEL), D_MODEL ** -0.5),
        'peer_wq': nrm((DEPTH, D_MODEL, PEER_HEADS * PEER_DQ), D_MODEL ** -0.5),
        'peer_keys1': nrm((DEPTH, PEER_HEADS, N_KEYS, PEER_DQ // 2), (PEER_DQ // 2) ** -0.5),
        'peer_keys2': nrm((DEPTH, PEER_HEADS, N_KEYS, PEER_DQ // 2), (PEER_DQ // 2) ** -0.5),
        'peer_u': nrm((DEPTH, N_EXPERTS, D_MODEL), D_MODEL ** -0.5),
        'peer_v': nrm((DEPTH, N_EXPERTS, D_MODEL), 0.1),
    }


def reference(x_prompt, x_sample, c, cache_ckv, cache_krope, state_ret_fwd, state_ret_bwd, c_ctx,
              w_mod, b_mod, norm_mix, norm_ffn, norm_final, w_in, q_norm, kv_norm, w_uq, w_ukv,
              ret_logit_fwd, ret_logit_bwd, ret_gn, w_up_a, w_up_b, w_o,
              peer_wq, peer_keys1, peer_keys2, peer_u, peer_v):
    xp, xs = x_prompt, x_sample
    new_ckv, new_kr, new_sf, new_sb = [], [], [], []
    for l in range(DEPTH):
        mod_c = jax.nn.silu(c_ctx) @ w_mod[l] + b_mod[l]
        mod_s = (jax.nn.silu(c) @ w_mod[l] + b_mod[l])[:, None, :]
        sh1c, sc1c, g1c, sh2c, sc2c, g2c = _split(mod_c, (D_MODEL,) * 6)
        sh1s, sc1s, g1s, sh2s, sc2s, g2s = _split(mod_s, (D_MODEL,) * 6)

        h = _modulate(_rmsnorm(xp, norm_mix[l]), sh1c, sc1c)
        qn, qr, ckv, kr, rq, rk, rv, rg, ga, gb = _project(h, w_in[l], q_norm[l], kv_norm[l], w_uq[l])
        kn, v = _expand_kv(ckv, w_ukv[l])
        attn = _mla_attend(qn, qr, kn, kr, v)
        zero = jnp.zeros((xp.shape[0], RET_HEADS, RET_DK, RET_DV), xp.dtype)
        ret, sf, sb = _retention(rq, rk, rv, rg, ret_logit_fwd[l], ret_logit_bwd[l], zero, zero, ret_gn[l])
        xp = xp + g1c * _merge(attn, ret, ga, gb, w_up_a[l], w_up_b[l], w_o[l])
        h = _modulate(_rmsnorm(xp, norm_ffn[l]), sh2c, sc2c)
        xp = xp + g2c * _peer(h, peer_wq[l], peer_keys1[l], peer_keys2[l], peer_u[l], peer_v[l])
        new_ckv.append(ckv)
        new_kr.append(kr)
        new_sf.append(sf)
        new_sb.append(sb)

        h = _modulate(_rmsnorm(xs, norm_mix[l]), sh1s, sc1s)
        qn, qr, ckv_s, kr_s, rq, rk, rv, rg, ga, gb = _project(h, w_in[l], q_norm[l], kv_norm[l], w_uq[l])
        qr = _rope_2d(qr)
        kr_s = _rope_2d(kr_s)
        ckv_all = jnp.concatenate([ckv_s, cache_ckv[:, l]], axis=1)
        kr_all = jnp.concatenate([kr_s, cache_krope[:, l]], axis=1)
        kn, v = _expand_kv(ckv_all, w_ukv[l])
        attn = _mla_attend(qn, qr, kn, kr_all, v)
        ret, _, _ = _retention(rq, rk, rv, rg, ret_logit_fwd[l], ret_logit_bwd[l],
                               state_ret_fwd[:, l], state_ret_bwd[:, l], ret_gn[l])
        xs = xs + g1s * _merge(attn, ret, ga, gb, w_up_a[l], w_up_b[l], w_o[l])
        h = _modulate(_rmsnorm(xs, norm_ffn[l]), sh2s, sc2s)
        xs = xs + g2s * _peer(h, peer_wq[l], peer_keys1[l], peer_keys2[l], peer_u[l], peer_v[l])

    y_prompt = _rmsnorm(xp, norm_final)
    y_sample = _rmsnorm(xs, norm_final)
    return (y_prompt, y_sample, jnp.stack(new_ckv, axis=1), jnp.stack(new_kr, axis=1),
            jnp.stack(new_sf, axis=1), jnp.stack(new_sb, axis=1))
```

```python
import functools

import jax
import jax.numpy as jnp
from jax import lax
from jax.experimental import pallas as pl
from jax.experimental.pallas import tpu as pltpu

F32 = jnp.float32
BF16 = jnp.bfloat16

D_MODEL = 1024
GRID_W = 64
EPS = 1e-6
MLA_HEADS = 8
MLA_NOPE = 64
MLA_ROPE = 32
MLA_V = 64
Q_LORA = 256
KV_LORA = 128
ROPE_BASE = 10000.0
RET_HEADS = 4
RET_DK = 128
RET_DV = 128
PEER_HEADS = 8
PEER_DQ = 256
N_KEYS = 128
PEER_TOPK = 16

LANE = 128
VMEM_LIMIT = 56 * 1024 * 1024

TM = 256
TQ = 128
RET_CHUNK = 256
TM_PEER = 512
E_CHUNK = 1024
MOD_ROWS = 16

_C_CQ = 0
_C_CKV = 256
_C_KR = 384
_C_KRS = 512
_C_RQ = 640
_C_RK = 1152
_C_RV = 1664
_C_RG = 2176
_C_GA = 2688
_C_GB = 3712
_D_IN_P = 4736


def _rms(x, g):
    return x * lax.rsqrt(jnp.mean(x * x, axis=-1, keepdims=True) + EPS) * g


def _dot(a, b):
    return jnp.dot(a, b, preferred_element_type=F32)


def _dot_nt(a, b):
    return lax.dot_general(a, b, (((1,), (1,)), ((), ())), preferred_element_type=F32)


def _mod_kernel(c_ref, w_ref, b_ref, o_ref):
    c = c_ref[...]
    s = c * jax.nn.sigmoid(c)
    o_ref[...] = _dot(s.astype(BF16), w_ref[...].astype(BF16)) + b_ref[...]


def _mod_call(cc, w_mod, b_mod):
    n = w_mod.shape[1]
    bn = 1024
    return pl.pallas_call(
        _mod_kernel,
        grid=(n // bn,),
        in_specs=[
            pl.BlockSpec((MOD_ROWS, D_MODEL), lambda j: (0, 0)),
            pl.BlockSpec((D_MODEL, bn), lambda j: (0, j)),
            pl.BlockSpec((1, bn), lambda j: (0, j)),
        ],
        out_specs=pl.BlockSpec((MOD_ROWS, bn), lambda j: (0, j)),
        out_shape=jax.ShapeDtypeStruct((MOD_ROWS, n), F32),
        compiler_params=pltpu.CompilerParams(dimension_semantics=("parallel",), vmem_limit_bytes=VMEM_LIMIT),
        name="mod",
    )(cc, w_mod, b_mod)


def _proj_kernel(x_ref, mod_ref, g_ref, win_ref, qn_ref, kvn_ref, wuq_ref, bdk_ref, ct_ref, st_ref,
                 qp_ref, kk_ref, ckv_ref, kr_ref, rq_ref, rk_ref, rv_ref, srg_ref, sga_ref, sgb_ref):
    x = x_ref[...]
    h = _rms(x, g_ref[...]) * (1.0 + mod_ref[0, 1:2, :]) + mod_ref[0, 0:1, :]
    y = _dot(h.astype(BF16), win_ref[...])

    ct = ct_ref[...]
    st = st_ref[...]
    ckv = _rms(y[:, _C_CKV:_C_CKV + KV_LORA], kvn_ref[...])
    kr = y[:, _C_KR:_C_KR + LANE]
    krot = kr * ct + y[:, _C_KRS:_C_KRS + LANE] * st
    ckv_ref[...] = ckv
    kr_ref[...] = kr
    kk_ref[:, 0:KV_LORA] = ckv.astype(BF16)
    kk_ref[:, KV_LORA:2 * KV_LORA] = krot.astype(BF16)

    cq = _rms(y[:, _C_CQ:_C_CQ + Q_LORA], qn_ref[...])
    qa = _dot(cq.astype(BF16), wuq_ref[...])
    nn = MLA_HEADS * MLA_NOPE
    qlat = _dot(qa[:, :nn].astype(BF16), bdk_ref[...])
    scale = (MLA_NOPE + MLA_ROPE) ** -0.5
    for hd in range(MLA_HEADS):
        lo = hd * LANE
        qrot = qa[:, nn + lo:nn + lo + LANE] * ct + qa[:, nn + MLA_HEADS * LANE + lo:nn + MLA_HEADS * LANE + lo + LANE] * st
        qp_ref[:, 2 * lo:2 * lo + LANE] = (qlat[:, lo:lo + LANE] * scale).astype(BF16)
        qp_ref[:, 2 * lo + LANE:2 * lo + 2 * LANE] = (qrot * scale).astype(BF16)

    hk = RET_HEADS * RET_DK
    rq_ref[...] = y[:, _C_RQ:_C_RQ + hk].astype(BF16)
    rk_ref[...] = (y[:, _C_RK:_C_RK + hk] * (RET_DK ** -0.5)).astype(BF16)
    rv_ref[...] = y[:, _C_RV:_C_RV + hk].astype(BF16)
    rg = y[:, _C_RG:_C_RG + hk]
    srg_ref[...] = rg * jax.nn.sigmoid(rg)
    sga_ref[...] = jax.nn.sigmoid(y[:, _C_GA:_C_GA + D_MODEL])
    sgb_ref[...] = jax.nn.sigmoid(y[:, _C_GB:_C_GB + D_MODEL])


def _proj_call(x, mod3, norm_mix, win_p, q_norm, kv_norm, wuq_all, bdk, ctab, stab, np_blocks, seq_blocks):
    n = x.shape[0]
    nblk = n // TM
    rope_id_block = ctab.shape[0] // TM - 1

    def mod_idx(i):
        return (jnp.where(i < np_blocks, 0, 1 + (i - np_blocks) // seq_blocks), 0, 0)

    def rope_idx(i):
        return (jnp.where(i < np_blocks, rope_id_block, (i - np_blocks) % seq_blocks), 0)

    const = lambda i: (0, 0)
    row = lambda i: (i, 0)
    hk = RET_HEADS * RET_DK
    out_shape = [
        jax.ShapeDtypeStruct((n, 2 * MLA_HEADS * LANE), BF16),
        jax.ShapeDtypeStruct((n, 2 * LANE), BF16),
        jax.ShapeDtypeStruct((n, KV_LORA), F32),
        jax.ShapeDtypeStruct((n, LANE), F32),
        jax.ShapeDtypeStruct((n, hk), BF16),
        jax.ShapeDtypeStruct((n, hk), BF16),
        jax.ShapeDtypeStruct((n, hk), BF16),
        jax.ShapeDtypeStruct((n, hk), F32),
        jax.ShapeDtypeStruct((n, D_MODEL), F32),
        jax.ShapeDtypeStruct((n, D_MODEL), F32),
    ]
    out_specs = [pl.BlockSpec((TM, s.shape[1]), row) for s in out_shape]
    return pl.pallas_call(
        _proj_kernel,
        grid=(nblk,),
        in_specs=[
            pl.BlockSpec((TM, D_MODEL), row),
            pl.BlockSpec((1, 8, D_MODEL), mod_idx),
            pl.BlockSpec((1, D_MODEL), const),
            pl.BlockSpec(win_p.shape, const),
            pl.BlockSpec((1, Q_LORA), const),
            pl.BlockSpec((1, KV_LORA), const),
            pl.BlockSpec(wuq_all.shape, const),
            pl.BlockSpec(bdk.shape, const),
            pl.BlockSpec((TM, LANE), rope_idx),
            pl.BlockSpec((TM, LANE), rope_idx),
        ],
        out_specs=out_specs,
        out_shape=out_shape,
        compiler_params=pltpu.CompilerParams(dimension_semantics=("parallel",), vmem_limit_bytes=VMEM_LIMIT),
        name="proj",
    )(x, mod3, norm_mix, win_p, q_norm, kv_norm, wuq_all, bdk, ctab, stab)


def _attn_kernel(q_ref, k_ref, bdv_ref, o_ref, qs_ref, m_ref, l_ref, acc_ref):
    ki = pl.program_id(2)
    nk = pl.num_programs(2)
    rows = MLA_HEADS * TQ

    @pl.when(ki == 0)
    def _():
        for hd in range(MLA_HEADS):
            qs_ref[hd * TQ:(hd + 1) * TQ, :] = q_ref[:, hd * 2 * LANE:(hd + 1) * 2 * LANE]
        m_ref[...] = jnp.full((rows, 1), -jnp.inf, F32)
        l_ref[...] = jnp.zeros((rows, 1), F32)
        acc_ref[...] = jnp.zeros((rows, KV_LORA), F32)

    k = k_ref[0]
    s = _dot_nt(qs_ref[...], k)
    m_prev = m_ref[...]
    m_new = jnp.maximum(m_prev, jnp.max(s, axis=1, keepdims=True))
    alpha = jnp.exp(m_prev - m_new)
    p = jnp.exp(s - m_new)
    l_ref[...] = alpha * l_ref[...] + jnp.sum(p, axis=1, keepdims=True)
    acc_ref[...] = alpha * acc_ref[...] + _dot(p.astype(BF16), k[:, :KV_LORA])
    m_ref[...] = m_new

    @pl.when(ki == nk - 1)
    def _():
        o = acc_ref[...] / l_ref[...]
        ocat = jnp.concatenate([o[hd * TQ:(hd + 1) * TQ, :] for hd in range(MLA_HEADS)], axis=1)
        o_ref[...] = _dot(ocat.astype(BF16), bdv_ref[...]).astype(BF16)


def _attn_call(qp, kk3, bdv, tok_off, batch, seq, tk):
    nq = seq // TQ
    nk = kk3.shape[1] // tk
    off = tok_off // TQ
    rows = MLA_HEADS * TQ
    return pl.pallas_call(
        _attn_kernel,
        grid=(batch, nq, nk),
        in_specs=[
            pl.BlockSpec((TQ, qp.shape[1]), lambda b, qi, ki: (off + b * nq + qi, 0)),
            pl.BlockSpec((1, tk, 2 * LANE), lambda b, qi, ki: (b, ki, 0)),
            pl.BlockSpec(bdv.shape, lambda b, qi, ki: (0, 0)),
        ],
        out_specs=pl.BlockSpec((TQ, MLA_HEADS * MLA_V), lambda b, qi, ki: (b * nq + qi, 0)),
        out_shape=jax.ShapeDtypeStruct((batch * seq, MLA_HEADS * MLA_V), BF16),
        scratch_shapes=[
            pltpu.VMEM((rows, 2 * LANE), BF16),
            pltpu.VMEM((rows, 1), F32),
            pltpu.VMEM((rows, 1), F32),
            pltpu.VMEM((rows, KV_LORA), F32),
        ],
        compiler_params=pltpu.CompilerParams(
            dimension_semantics=("parallel", "parallel", "arbitrary"), vmem_limit_bytes=VMEM_LIMIT),
        name="attn_b%d" % batch,
    )(qp, kk3, bdv)


def _log_sigmoid(x):
    return jnp.minimum(x, 0.0) - jnp.log(1.0 + jnp.exp(-jnp.abs(x)))


def _ret_kernel(has_state, seq, *refs):
    if has_state:
        (q_ref, k_ref, v_ref, srg_ref, gn_ref, lf_ref, lb_ref, s0f_ref, s0b_ref,
         o_ref, sf_ref, sb_ref, yf_ref, yb_ref) = refs
    else:
        (q_ref, k_ref, v_ref, srg_ref, gn_ref, lf_ref, lb_ref,
         o_ref, sf_ref, sb_ref, yf_ref, yb_ref) = refs
    c = RET_CHUNK
    n = seq // c
    lgf = _log_sigmoid(lf_ref[0, :, 0:1])
    lgb = _log_sigmoid(lb_ref[0, :, 0:1])
    ii = lax.broadcasted_iota(jnp.int32, (c, c), 0)
    jj = lax.broadcasted_iota(jnp.int32, (c, c), 1)
    diff = (ii - jj).astype(F32)
    dmat_f = jnp.where(diff >= 0, jnp.exp(lgf * jnp.maximum(diff, 0.0)), 0.0)
    dmat_b = jnp.where(diff <= 0, jnp.exp(lgb * jnp.maximum(-diff, 0.0)), 0.0)
    idx = lax.broadcasted_iota(jnp.int32, (c, 1), 0).astype(F32)
    qdec_f = jnp.exp(lgf * (idx + 1.0))
    kdec_f = jnp.exp(lgf * (c - 1.0 - idx))
    cdec_f = jnp.exp(lgf * c)
    qdec_b = jnp.exp(lgb * (c - idx))
    kdec_b = jnp.exp(lgb * idx)
    cdec_b = jnp.exp(lgb * c)

    def chunk(start, state, dmat, qdec, kdec, cdec, y_ref):
        q = q_ref[pl.ds(start, c), :]
        k = k_ref[pl.ds(start, c), :]
        v = v_ref[pl.ds(start, c), :]
        sc = _dot_nt(q, k) * dmat
        y = _dot(sc.astype(BF16), v) + _dot(q, state.astype(BF16)) * qdec
        y_ref[pl.ds(start, c), :] = y
        kdt = (k.astype(F32) * kdec).T.astype(BF16)
        return state * cdec + _dot(kdt, v)

    def body(t, carry):
        sf, sb = carry
        sf = chunk(pl.multiple_of(t * c, c), sf, dmat_f, qdec_f, kdec_f, cdec_f, yf_ref)
        sb = chunk(pl.multiple_of((n - 1 - t) * c, c), sb, dmat_b, qdec_b, kdec_b, cdec_b, yb_ref)
        return sf, sb

    if has_state:
        init = (s0f_ref[0, 0], s0b_ref[0, 0])
    else:
        init = (jnp.zeros((RET_DK, RET_DV), F32), jnp.zeros((RET_DK, RET_DV), F32))
    sf, sb = lax.fori_loop(0, n, body, init)
    sf_ref[0, 0] = sf
    sb_ref[0, 0] = sb

    y = yf_ref[...] + yb_ref[...]
    mu = jnp.mean(y, axis=-1, keepdims=True)
    yc = y - mu
    var = jnp.mean(yc * yc, axis=-1, keepdims=True)
    yn = yc * lax.rsqrt(var + EPS)
    o_ref[...] = (srg_ref[...] * (yn * gn_ref[...])).astype(BF16)


def _ret_call(rq, rk, rv, srg, gn, lf, lb, s0f, s0b, tok_off, batch, seq):
    off = tok_off // seq
    has_state = s0f is not None
    tokblk = lambda b, h: (off + b, h)
    st = lambda b, h: (b, h, 0, 0)
    in_specs = [
        pl.BlockSpec((seq, RET_DK), tokblk),
        pl.BlockSpec((seq, RET_DK), tokblk),
        pl.BlockSpec((seq, RET_DV), tokblk),
        pl.BlockSpec((seq, RET_DV), tokblk),
        pl.BlockSpec((1, RET_DV), lambda b, h: (0, h)),
        pl.BlockSpec((1, 1, LANE), lambda b, h: (h, 0, 0)),
        pl.BlockSpec((1, 1, LANE), lambda b, h: (h, 0, 0)),
    ]
    args = [rq, rk, rv, srg, gn, lf, lb]
    if has_state:
        in_specs += [pl.BlockSpec((1, 1, RET_DK, RET_DV), st)] * 2
        args += [s0f, s0b]
    st_shape = jax.ShapeDtypeStruct((batch, RET_HEADS, RET_DK, RET_DV), F32)
    return pl.pallas_call(
        functools.partial(_ret_kernel, has_state, seq),
        grid=(batch, RET_HEADS),
        in_specs=in_specs,
        out_specs=[
            pl.BlockSpec((seq, RET_DV), lambda b, h: (b, h)),
            pl.BlockSpec((1, 1, RET_DK, RET_DV), st),
            pl.BlockSpec((1, 1, RET_DK, RET_DV), st),
        ],
        out_shape=[jax.ShapeDtypeStruct((batch * seq, RET_HEADS * RET_DV), BF16), st_shape, st_shape],
        scratch_shapes=[pltpu.VMEM((seq, RET_DV), F32), pltpu.VMEM((seq, RET_DV), F32)],
        compiler_params=pltpu.CompilerParams(
            dimension_semantics=("parallel", "parallel"), vmem_limit_bytes=VMEM_LIMIT),
        name="ret_b%d" % batch,
    )(*args)


def _merge_kernel(np_blocks, ap_ref, as_ref, rp_ref, rs_ref, sga_ref, sgb_ref, x_ref, mod_ref, g_ref,
                  wa_ref, wb_ref, wo_ref, x1_ref, h2t_ref):
    is_prompt = pl.program_id(0) < np_blocks
    attn = jnp.where(is_prompt, ap_ref[...], as_ref[...])
    ret = jnp.where(is_prompt, rp_ref[...], rs_ref[...])
    m = sga_ref[...] * _dot(attn, wa_ref[...]) + sgb_ref[...] * _dot(ret, wb_ref[...])
    o = _dot(m.astype(BF16), wo_ref[...])
    x1 = x_ref[...] + mod_ref[0, 2:3, :] * o
    x1_ref[...] = x1
    h2 = _rms(x1, g_ref[...]) * (1.0 + mod_ref[0, 4:5, :]) + mod_ref[0, 3:4, :]
    h2t_ref[...] = h2.T.astype(BF16)


def _merge_call(attn_p, attn_s, ret_p, ret_s, sga, sgb, x, mod3, norm_ffn, wa, wb, wo, np_blocks, seq_blocks):
    n = x.shape[0]
    nblk = n // TM
    const = lambda i: (0, 0)
    row = lambda i: (i, 0)
    p_idx = lambda i: (jnp.minimum(i, np_blocks - 1), 0)
    s_idx = lambda i: (jnp.maximum(i - np_blocks, 0), 0)

    def mod_idx(i):
        return (jnp.where(i < np_blocks, 0, 1 + (i - np_blocks) // seq_blocks), 0, 0)

    w = attn_p.shape[1]
    return pl.pallas_call(
        functools.partial(_merge_kernel, np_blocks),
        grid=(nblk,),
        in_specs=[
            pl.BlockSpec((TM, w), p_idx),
            pl.BlockSpec((TM, w), s_idx),
            pl.BlockSpec((TM, w), p_idx),
            pl.BlockSpec((TM, w), s_idx),
            pl.BlockSpec((TM, D_MODEL), row),
            pl.BlockSpec((TM, D_MODEL), row),
            pl.BlockSpec((TM, D_MODEL), row),
            pl.BlockSpec((1, 8, D_MODEL), mod_idx),
            pl.BlockSpec((1, D_MODEL), const),
            pl.BlockSpec(wa.shape, const),
            pl.BlockSpec(wb.shape, const),
            pl.BlockSpec(wo.shape, const),
        ],
        out_specs=[pl.BlockSpec((TM, D_MODEL), row), pl.BlockSpec((D_MODEL, TM), lambda i: (0, i))],
        out_shape=[jax.ShapeDtypeStruct((n, D_MODEL), F32), jax.ShapeDtypeStruct((D_MODEL, n), BF16)],
        compiler_params=pltpu.CompilerParams(dimension_semantics=("parallel",), vmem_limit_bytes=VMEM_LIMIT),
        name="merge",
    )(attn_p, attn_s, ret_p, ret_s, sga, sgb, x, mod3, norm_ffn, wa, wb, wo)


def _top_rows(s, k):
    rows = []
    for _ in range(k):
        m = jnp.max(s, axis=0, keepdims=True)
        rows.append(m)
        s = jnp.where(s == m, -jnp.inf, s)
    return rows


def _stack_rows(rows):
    n, cols = len(rows), rows[0].shape[1]
    ridx = lax.broadcasted_iota(jnp.int32, (n, cols), 0)
    out = jnp.broadcast_to(rows[0], (n, cols))
    for a in range(1, n):
        out = jnp.where(ridx == a, rows[a], out)
    return out


def _peer_route(qt_ref, k1_ref, k2_ref, s2_ref, e2_ref, th_ref, cc_ref):
    half = PEER_DQ // 2
    for hd in range(PEER_HEADS):
        kr = slice(hd * N_KEYS, (hd + 1) * N_KEYS)
        q1 = qt_ref[hd * PEER_DQ:hd * PEER_DQ + half, :].astype(BF16)
        q2 = qt_ref[hd * PEER_DQ + half:(hd + 1) * PEER_DQ, :].astype(BF16)
        s1 = _dot(k1_ref[kr, :], q1)
        s2 = _dot(k2_ref[kr, :], q2)
        r1 = _top_rows(s1, PEER_TOPK)
        r2 = _top_rows(s2, PEER_TOPK)
        v1 = _stack_rows(r1)
        v2 = _stack_rows(r2)
        k8 = PEER_TOPK // 2
        cands = [r1[0] + v2, r1[1] + v2[0:k8, :], r1[2] + v2[0:k8, :], r1[3] + v2[0:k8, :],
                 v1 + r2[0], v1[0:k8, :] + r2[1], v1[0:k8, :] + r2[2]]
        tau = _top_rows(jnp.concatenate(cands, axis=0), PEER_TOPK)[-1]
        e1 = [jnp.exp(r - r1[0]) for r in r1]
        e2v = jnp.exp(v2 - r2[0])
        ths, zs = [], []
        for a in range(PEER_TOPK):
            sel = (r1[a] + v2) >= tau
            ths.append(jnp.min(jnp.where(sel, v2, jnp.inf), axis=0, keepdims=True))
            zs.append(jnp.sum(jnp.where(sel, e2v, 0.0), axis=0, keepdims=True))
        z = e1[0] * zs[0]
        for a in range(1, PEER_TOPK):
            z = z + e1[a] * zs[a]
        zinv = 1.0 / z
        th = jnp.full(s1.shape, jnp.inf, F32)
        cc = jnp.zeros(s1.shape, F32)
        for a in range(PEER_TOPK):
            eq = s1 == r1[a]
            th = jnp.where(eq, ths[a], th)
            cc = jnp.where(eq, e1[a] * zinv, cc)
        s2_ref[kr, :] = s2
        e2_ref[kr, :] = jnp.exp(s2 - r2[0])
        th_ref[kr, :] = th
        cc_ref[kr, :] = cc


def _peer_kernel(h2t_ref, x1_ref, mod_ref, g_ref, wqt_ref, k1_ref, k2_ref, u_ref, vt_ref,
                 y_ref, qt_ref, s2_ref, e2_ref, th_ref, cc_ref, acc_ref):
    j = pl.program_id(1)
    nj = pl.num_programs(1)

    @pl.when(j == 0)
    def _():
        qt_ref[...] = _dot(wqt_ref[...], h2t_ref[...])
        _peer_route(qt_ref, k1_ref, k2_ref, s2_ref, e2_ref, th_ref, cc_ref)
        acc_ref[...] = jnp.zeros(acc_ref.shape, F32)

    hu = _dot(u_ref[...], h2t_ref[...])
    act = jax.nn.gelu(hu, approximate=True)
    parts = []
    for il in range(E_CHUNK // N_KEYS):
        g = None
        for hd in range(PEER_HEADS):
            r = hd * N_KEYS + j * (E_CHUNK // N_KEYS) + il
            th = th_ref[pl.ds(r, 1), :]
            cc = cc_ref[pl.ds(r, 1), :]
            kr = slice(hd * N_KEYS, (hd + 1) * N_KEYS)
            t = jnp.where(s2_ref[kr, :] >= th, e2_ref[kr, :], 0.0) * cc
            g = t if g is None else g + t
        parts.append((g * act[il * N_KEYS:(il + 1) * N_KEYS, :]).astype(BF16))
    p = jnp.concatenate(parts, axis=0)
    acc_ref[...] += _dot(vt_ref[...], p)

    @pl.when(j == nj - 1)
    def _():
        x2 = x1_ref[...] + mod_ref[0, 5:6, :] * acc_ref[...].T
        y_ref[...] = _rms(x2, g_ref[...])


def _peer_call(h2t, x1, mod3, norm_final, wqt, k1, k2, u_b, vt_b, np_tokens, seq):
    n = x1.shape[0]
    tm = TM_PEER
    nblk = n // tm
    npb = np_tokens // tm
    sb = seq // tm
    ne = u_b.shape[0] // E_CHUNK
    nq = PEER_HEADS * PEER_DQ
    nk = PEER_HEADS * N_KEYS

    def mod_idx(i, j):
        return (jnp.where(i < npb, 0, 1 + (i - npb) // sb), 0, 0)

    const = lambda i, j: (0, 0)
    return pl.pallas_call(
        _peer_kernel,
        grid=(nblk, ne),
        in_specs=[
            pl.BlockSpec((D_MODEL, tm), lambda i, j: (0, i)),
            pl.BlockSpec((tm, D_MODEL), lambda i, j: (i, 0)),
            pl.BlockSpec((1, 8, D_MODEL), mod_idx),
            pl.BlockSpec((1, D_MODEL), const),
            pl.BlockSpec(wqt.shape, const),
            pl.BlockSpec(k1.shape, const),
            pl.BlockSpec(k2.shape, const),
            pl.BlockSpec((E_CHUNK, D_MODEL), lambda i, j: (j, 0)),
            pl.BlockSpec((D_MODEL, E_CHUNK), lambda i, j: (0, j)),
        ],
        out_specs=pl.BlockSpec((tm, D_MODEL), lambda i, j: (i, 0)),
        out_shape=jax.ShapeDtypeStruct((n, D_MODEL), F32),
        scratch_shapes=[
            pltpu.VMEM((nq, tm), F32),
            pltpu.VMEM((nk, tm), F32),
            pltpu.VMEM((nk, tm), F32),
            pltpu.VMEM((nk, tm), F32),
            pltpu.VMEM((nk, tm), F32),
            pltpu.VMEM((D_MODEL, tm), F32),
        ],
        compiler_params=pltpu.CompilerParams(
            dimension_semantics=("parallel", "arbitrary"), vmem_limit_bytes=VMEM_LIMIT),
        name="peer",
    )(h2t, x1, mod3, norm_final, wqt, k1, k2, u_b, vt_b)


def _rope_tables(seq):
    rows = seq // GRID_W
    row = jnp.repeat(jnp.arange(rows, dtype=F32), GRID_W)
    col = jnp.tile(jnp.arange(GRID_W, dtype=F32), rows)
    nf = MLA_ROPE // 4
    freqs = jnp.power(ROPE_BASE, -jnp.arange(nf, dtype=F32) / nf)
    ang = jnp.concatenate([row[:, None] * freqs, col[:, None] * freqs], axis=-1)
    cos, sin = jnp.cos(ang), jnp.sin(ang)
    pad = jnp.zeros((seq, LANE - MLA_ROPE), F32)
    ctab = jnp.concatenate([cos, cos, pad], axis=1)
    stab = jnp.concatenate([-sin, sin, pad], axis=1)
    cid = jnp.concatenate([jnp.ones((TM, MLA_ROPE), F32), jnp.zeros((TM, LANE - MLA_ROPE), F32)], axis=1)
    ctab = jnp.concatenate([ctab, cid], axis=0)
    stab = jnp.concatenate([stab, jnp.zeros((TM, LANE), F32)], axis=0)
    return ctab, stab


def _pad_cols(a, width):
    return jnp.concatenate([a, jnp.zeros((a.shape[0], width - a.shape[1]), a.dtype)], axis=1)


def kernel(x_prompt, x_sample, c, cache_ckv, cache_krope, state_ret_fwd, state_ret_bwd, c_ctx, w_mod, b_mod, norm_mix, norm_ffn, norm_final, w_in, q_norm, kv_norm, w_uq, w_ukv, ret_logit_fwd, ret_logit_bwd, ret_gn, w_up_a, w_up_b, w_o, peer_wq, peer_keys1, peer_keys2, peer_u, peer_v):
    depth = w_mod.shape[0]
    assert depth == 1
    bp, sp, _ = x_prompt.shape
    bs, ss, _ = x_sample.shape
    past = cache_ckv.shape[2]
    n_p, n_s = bp * sp, bs * ss
    assert 1 + bs <= MOD_ROWS and sp == TM and ss % TM == 0 and ss % TM_PEER == 0 and n_p % TM_PEER == 0
    np_blocks, seq_blocks = n_p // TM, ss // TM
    l = 0

    half = MLA_ROPE // 2
    kr1 = w_in[l][:, 384:384 + half]
    kr2 = w_in[l][:, 384 + half:384 + MLA_ROPE]
    win_p = jnp.concatenate([
        w_in[l][:, :384],
        _pad_cols(jnp.concatenate([kr1, kr2], axis=1), LANE),
        _pad_cols(jnp.concatenate([kr2, kr1], axis=1), LANE),
        w_in[l][:, 384 + MLA_ROPE:],
    ], axis=1).astype(BF16)
    assert win_p.shape[1] == _D_IN_P
    w3 = w_uq[l].reshape(Q_LORA, MLA_HEADS, MLA_NOPE + MLA_ROPE)
    r1 = w3[:, :, MLA_NOPE:MLA_NOPE + half]
    r2 = w3[:, :, MLA_NOPE + half:]
    zpad = jnp.zeros((Q_LORA, MLA_HEADS, LANE - MLA_ROPE), F32)
    wuq_all = jnp.concatenate([
        w3[:, :, :MLA_NOPE].reshape(Q_LORA, MLA_HEADS * MLA_NOPE),
        jnp.concatenate([r1, r2, zpad], axis=2).reshape(Q_LORA, MLA_HEADS * LANE),
        jnp.concatenate([r2, r1, zpad], axis=2).reshape(Q_LORA, MLA_HEADS * LANE),
    ], axis=1).astype(BF16)
    wkv3 = w_ukv[l].reshape(KV_LORA, MLA_HEADS, MLA_NOPE + MLA_V)
    eye = jnp.eye(MLA_HEADS, dtype=F32)
    wk_hdl = jnp.transpose(wkv3[:, :, :MLA_NOPE], (1, 2, 0))
    wv_hld = jnp.transpose(wkv3[:, :, MLA_NOPE:], (1, 0, 2))
    bdk = (wk_hdl[:, :, None, :] * eye[:, None, :, None]).reshape(
        MLA_HEADS * MLA_NOPE, MLA_HEADS * KV_LORA).astype(BF16)
    bdv = (wv_hld[:, :, None, :] * eye[:, None, :, None]).reshape(
        MLA_HEADS * KV_LORA, MLA_HEADS * MLA_V).astype(BF16)
    wa = w_up_a[l].astype(BF16)
    wb = w_up_b[l].astype(BF16)
    wo = w_o[l].astype(BF16)
    wqt = peer_wq[l].T.astype(BF16)
    k1 = peer_keys1[l].reshape(PEER_HEADS * N_KEYS, PEER_DQ // 2).astype(BF16)
    k2 = peer_keys2[l].reshape(PEER_HEADS * N_KEYS, PEER_DQ // 2).astype(BF16)
    u_b = peer_u[l].astype(BF16)
    vt_b = peer_v[l].T.astype(BF16)
    ctab, stab = _rope_tables(ss)
    lf = jnp.broadcast_to(ret_logit_fwd[l][:, None, None], (RET_HEADS, 1, LANE))
    lb = jnp.broadcast_to(ret_logit_bwd[l][:, None, None], (RET_HEADS, 1, LANE))

    cc = jnp.concatenate([c_ctx[None, :], c, jnp.zeros((MOD_ROWS - 1 - bs, D_MODEL), F32)], axis=0)
    mod = _mod_call(cc, w_mod[l], b_mod[l][None, :])
    mod3 = jnp.concatenate([mod.reshape(MOD_ROWS, 6, D_MODEL), jnp.zeros((MOD_ROWS, 2, D_MODEL), F32)], axis=1)

    x = jnp.concatenate([x_prompt.reshape(n_p, D_MODEL), x_sample.reshape(n_s, D_MODEL)], axis=0)
    qp, kk, ckv32, kr32, rq, rk, rv, srg, sga, sgb = _proj_call(
        x, mod3, norm_mix[l][None, :], win_p, q_norm[l][None, :], kv_norm[l][None, :], wuq_all, bdk,
        ctab, stab, np_blocks, seq_blocks)

    kk_p = kk[:n_p].reshape(bp, sp, 2 * LANE)
    cache_kk = jnp.concatenate([cache_ckv[:, l], _pad_cols(
        cache_krope[:, l].reshape(bs * past, MLA_ROPE), LANE).reshape(bs, past, LANE)], axis=2).astype(BF16)
    kk_s = jnp.concatenate([kk[n_p:].reshape(bs, ss, 2 * LANE), cache_kk], axis=1)
    attn_p = _attn_call(qp, kk_p, bdv, 0, bp, sp, sp)
    attn_s = _attn_call(qp, kk_s, bdv, n_p, bs, ss, 512)

    gn = ret_gn[l][None, :]
    ret_p, sf, sb = _ret_call(rq, rk, rv, srg, gn, lf, lb, None, None, 0, bp, sp)
    ret_s, _, _ = _ret_call(rq, rk, rv, srg, gn, lf, lb, state_ret_fwd[:, l], state_ret_bwd[:, l], n_p, bs, ss)

    x1, h2t = _merge_call(attn_p, attn_s, ret_p, ret_s, sga, sgb, x, mod3, norm_ffn[l][None, :],
                          wa, wb, wo, np_blocks, seq_blocks)

    y = _peer_call(h2t, x1, mod3, norm_final[None, :], wqt, k1, k2, u_b, vt_b, n_p, ss)

    y_prompt = y[:n_p].reshape(bp, sp, D_MODEL)
    y_sample = y[n_p:].reshape(bs, ss, D_MODEL)
    new_ckv = ckv32[:n_p].reshape(bp, 1, sp, KV_LORA)
    new_kr = kr32[:n_p, :MLA_ROPE].reshape(bp, 1, sp, MLA_ROPE)
    return (y_prompt, y_sample, new_ckv, new_kr, sf[:, None], sb[:, None])
```

```python
import functools

import jax
import jax.numpy as jnp
from jax import lax
from jax.experimental import pallas as pl
from jax.experimental.pallas import tpu as pltpu

F32 = jnp.float32
BF16 = jnp.bfloat16

D_MODEL = 1024
GRID_W = 64
EPS = 1e-6
MLA_HEADS = 8
MLA_NOPE = 64
MLA_ROPE = 32
MLA_V = 64
Q_LORA = 256
KV_LORA = 128
ROPE_BASE = 10000.0
RET_HEADS = 4
RET_DK = 128
RET_DV = 128
PEER_HEADS = 8
PEER_DQ = 256
N_KEYS = 128
PEER_TOPK = 16

LANE = 128
VMEM_LIMIT = 56 * 1024 * 1024

TM = 256
TQ = 128
RET_CHUNK = 256
TM_PEER = 512
E_CHUNK = 1024
MOD_ROWS = 16

_C_CQ = 0
_C_CKV = 256
_C_KR = 384
_C_KRS = 512
_C_RQ = 640
_C_RK = 1152
_C_RV = 1664
_C_RG = 2176
_C_GA = 2688
_C_GB = 3712
_D_IN_P = 4736


def _rms(x, g):
    return x * lax.rsqrt(jnp.mean(x * x, axis=-1, keepdims=True) + EPS) * g


def _dot(a, b):
    return jnp.dot(a, b, preferred_element_type=F32)


def _dot_nt(a, b):
    return lax.dot_general(a, b, (((1,), (1,)), ((), ())), preferred_element_type=F32)


def _mod_kernel(c_ref, w_ref, b_ref, o_ref):
    c = c_ref[...]
    s = c * jax.nn.sigmoid(c)
    o_ref[...] = _dot(s.astype(BF16), w_ref[...].astype(BF16)) + b_ref[...]


def _mod_call(cc, w_mod, b_mod):
    n = w_mod.shape[1]
    bn = 1024
    return pl.pallas_call(
        _mod_kernel,
        grid=(n // bn,),
        in_specs=[
            pl.BlockSpec((MOD_ROWS, D_MODEL), lambda j: (0, 0)),
            pl.BlockSpec((D_MODEL, bn), lambda j: (0, j)),
            pl.BlockSpec((1, bn), lambda j: (0, j)),
        ],
        out_specs=pl.BlockSpec((MOD_ROWS, bn), lambda j: (0, j)),
        out_shape=jax.ShapeDtypeStruct((MOD_ROWS, n), F32),
        compiler_params=pltpu.CompilerParams(dimension_semantics=("parallel",), vmem_limit_bytes=VMEM_LIMIT),
        name="mod",
    )(cc, w_mod, b_mod)


def _proj_kernel(x_ref, mod_ref, g_ref, win_ref, qn_ref, kvn_ref, wuqt_ref, bdkt_ref, ct_ref, st_ref, ctt_ref, stt_ref,
                 qt_ref, kk_ref, ckvt_ref, ckv_ref, kr_ref, rq_ref, rk_ref, rv_ref, srg_ref, sga_ref, sgb_ref):
    x = x_ref[...]
    h = _rms(x, g_ref[...]) * (1.0 + mod_ref[0, 1:2, :]) + mod_ref[0, 0:1, :]
    y = _dot(h.astype(BF16), win_ref[...])

    ct = ct_ref[...]
    st = st_ref[...]
    ckv = _rms(y[:, _C_CKV:_C_CKV + KV_LORA], kvn_ref[...])
    kr = y[:, _C_KR:_C_KR + LANE]
    krot = kr * ct + y[:, _C_KRS:_C_KRS + LANE] * st
    ckv_ref[...] = ckv
    kr_ref[...] = kr
    kk_ref[:, 0:KV_LORA] = ckv.astype(BF16)
    kk_ref[:, KV_LORA:2 * KV_LORA] = krot.astype(BF16)
    ckvt_ref[...] = ckv.T.astype(BF16)

    cq = _rms(y[:, _C_CQ:_C_CQ + Q_LORA], qn_ref[...])
    qat = _dot(wuqt_ref[...], cq.T.astype(BF16))
    nn = MLA_HEADS * MLA_NOPE
    nr = MLA_HEADS * LANE
    qlat = _dot(bdkt_ref[...], qat[:nn, :].astype(BF16))
    ctt = ctt_ref[...]
    stt = stt_ref[...]
    scale = (MLA_NOPE + MLA_ROPE) ** -0.5
    for hd in range(MLA_HEADS):
        lo = hd * LANE
        qrot = qat[nn + lo:nn + lo + LANE, :] * ctt + qat[nn + nr + lo:nn + nr + lo + LANE, :] * stt
        qt_ref[2 * lo:2 * lo + LANE, :] = (qlat[lo:lo + LANE, :] * scale).astype(BF16)
        qt_ref[2 * lo + LANE:2 * lo + 2 * LANE, :] = (qrot * scale).astype(BF16)

    hk = RET_HEADS * RET_DK
    rq_ref[...] = y[:, _C_RQ:_C_RQ + hk].astype(BF16)
    rk_ref[...] = (y[:, _C_RK:_C_RK + hk] * (RET_DK ** -0.5)).astype(BF16)
    rv_ref[...] = y[:, _C_RV:_C_RV + hk].astype(BF16)
    rg = y[:, _C_RG:_C_RG + hk]
    srg_ref[...] = rg * jax.nn.sigmoid(rg)
    sga_ref[...] = jax.nn.sigmoid(y[:, _C_GA:_C_GA + D_MODEL])
    sgb_ref[...] = jax.nn.sigmoid(y[:, _C_GB:_C_GB + D_MODEL])


def _proj_call(x, mod3, norm_mix, win_p, q_norm, kv_norm, wuq_t, bdk_t, ctab, stab, np_blocks, seq_blocks):
    n = x.shape[0]
    nblk = n // TM
    rope_id_block = ctab.shape[0] // TM - 1

    def mod_idx(i):
        return (jnp.where(i < np_blocks, 0, 1 + (i - np_blocks) // seq_blocks), 0, 0)

    def rope_blk(i):
        return jnp.where(i < np_blocks, rope_id_block, (i - np_blocks) % seq_blocks)

    rope_idx = lambda i: (rope_blk(i), 0)
    rope_idx_t = lambda i: (0, rope_blk(i))
    const = lambda i: (0, 0)
    row = lambda i: (i, 0)
    col = lambda i: (0, i)
    hk = RET_HEADS * RET_DK
    out_shape = [
        jax.ShapeDtypeStruct((2 * MLA_HEADS * LANE, n), BF16),
        jax.ShapeDtypeStruct((n, 2 * LANE), BF16),
        jax.ShapeDtypeStruct((KV_LORA, n), BF16),
        jax.ShapeDtypeStruct((n, KV_LORA), F32),
        jax.ShapeDtypeStruct((n, LANE), F32),
        jax.ShapeDtypeStruct((n, hk), BF16),
        jax.ShapeDtypeStruct((n, hk), BF16),
        jax.ShapeDtypeStruct((n, hk), BF16),
        jax.ShapeDtypeStruct((n, hk), F32),
        jax.ShapeDtypeStruct((n, D_MODEL), F32),
        jax.ShapeDtypeStruct((n, D_MODEL), F32),
    ]
    out_specs = [pl.BlockSpec((s.shape[0], TM), col) if s.shape[1] == n else pl.BlockSpec((TM, s.shape[1]), row)
                 for s in out_shape]
    return pl.pallas_call(
        _proj_kernel,
        grid=(nblk,),
        in_specs=[
            pl.BlockSpec((TM, D_MODEL), row),
            pl.BlockSpec((1, 8, D_MODEL), mod_idx),
            pl.BlockSpec((1, D_MODEL), const),
            pl.BlockSpec(win_p.shape, const),
            pl.BlockSpec((1, Q_LORA), const),
            pl.BlockSpec((1, KV_LORA), const),
            pl.BlockSpec(wuq_t.shape, const),
            pl.BlockSpec(bdk_t.shape, const),
            pl.BlockSpec((TM, LANE), rope_idx),
            pl.BlockSpec((TM, LANE), rope_idx),
            pl.BlockSpec((LANE, TM), rope_idx_t),
            pl.BlockSpec((LANE, TM), rope_idx_t),
        ],
        out_specs=out_specs,
        out_shape=out_shape,
        compiler_params=pltpu.CompilerParams(dimension_semantics=("parallel",), vmem_limit_bytes=VMEM_LIMIT),
        name="proj",
    )(x, mod3, norm_mix, win_p, q_norm, kv_norm, wuq_t, bdk_t, ctab, stab, ctab.T, stab.T)


def _attn_kernel(nkb, tkb, qt_ref, k_ref, vt_ref, bdv_ref, o_ref, s_ref):
    fw = 2 * LANE
    qt = jnp.concatenate([qt_ref[hd * fw:(hd + 1) * fw, :] for hd in range(MLA_HEADS)], axis=1)
    m = None
    for kb in range(nkb):
        s = _dot(k_ref[0, kb * tkb:(kb + 1) * tkb, :], qt)
        s_ref[kb * tkb:(kb + 1) * tkb, :] = s
        mk = jnp.max(s, axis=0, keepdims=True)
        m = mk if m is None else jnp.maximum(m, mk)
    l = None
    acc = None
    for kb in range(nkb):
        p = jnp.exp(s_ref[kb * tkb:(kb + 1) * tkb, :] - m)
        lk = jnp.sum(p, axis=0, keepdims=True)
        ak = _dot(vt_ref[0, :, kb * tkb:(kb + 1) * tkb], p.astype(BF16))
        l = lk if l is None else l + lk
        acc = ak if acc is None else acc + ak
    o = acc / l
    ocat = jnp.concatenate([o[:, hd * TQ:(hd + 1) * TQ].T for hd in range(MLA_HEADS)], axis=1)
    o_ref[...] = _dot(ocat.astype(BF16), bdv_ref[...]).astype(BF16)


def _attn_call(qt, kk3, vt3, bdv, tok_off, batch, seq, tkb):
    nq = seq // TQ
    tk = kk3.shape[1]
    off = tok_off // TQ
    return pl.pallas_call(
        functools.partial(_attn_kernel, tk // tkb, tkb),
        grid=(batch, nq),
        in_specs=[
            pl.BlockSpec((qt.shape[0], TQ), lambda b, qi: (0, off + b * nq + qi)),
            pl.BlockSpec((1, tk, 2 * LANE), lambda b, qi: (b, 0, 0)),
            pl.BlockSpec((1, KV_LORA, tk), lambda b, qi: (b, 0, 0)),
            pl.BlockSpec(bdv.shape, lambda b, qi: (0, 0)),
        ],
        out_specs=pl.BlockSpec((TQ, MLA_HEADS * MLA_V), lambda b, qi: (b * nq + qi, 0)),
        out_shape=jax.ShapeDtypeStruct((batch * seq, MLA_HEADS * MLA_V), BF16),
        scratch_shapes=[pltpu.VMEM((tk, MLA_HEADS * TQ), F32)],
        compiler_params=pltpu.CompilerParams(
            dimension_semantics=("parallel", "parallel"), vmem_limit_bytes=VMEM_LIMIT),
        name="attn_b%d" % batch,
    )(qt, kk3, vt3, bdv)


def _log_sigmoid(x):
    return jnp.minimum(x, 0.0) - jnp.log(1.0 + jnp.exp(-jnp.abs(x)))


def _ret_kernel(has_state, seq, *refs):
    if has_state:
        (q_ref, k_ref, v_ref, srg_ref, gn_ref, lf_ref, lb_ref, s0f_ref, s0b_ref,
         o_ref, sf_ref, sb_ref, yf_ref, yb_ref) = refs
    else:
        (q_ref, k_ref, v_ref, srg_ref, gn_ref, lf_ref, lb_ref,
         o_ref, sf_ref, sb_ref, yf_ref, yb_ref) = refs
    c = RET_CHUNK
    n = seq // c
    lgf = _log_sigmoid(lf_ref[0, :, 0:1])
    lgb = _log_sigmoid(lb_ref[0, :, 0:1])
    ii = lax.broadcasted_iota(jnp.int32, (c, c), 0)
    jj = lax.broadcasted_iota(jnp.int32, (c, c), 1)
    diff = (ii - jj).astype(F32)
    dmat_f = jnp.where(diff >= 0, jnp.exp(lgf * jnp.maximum(diff, 0.0)), 0.0)
    dmat_b = jnp.where(diff <= 0, jnp.exp(lgb * jnp.maximum(-diff, 0.0)), 0.0)
    idx = lax.broadcasted_iota(jnp.int32, (c, 1), 0).astype(F32)
    qdec_f = jnp.exp(lgf * (idx + 1.0))
    kdec_f = jnp.exp(lgf * (c - 1.0 - idx))
    cdec_f = jnp.exp(lgf * c)
    qdec_b = jnp.exp(lgb * (c - idx))
    kdec_b = jnp.exp(lgb * idx)
    cdec_b = jnp.exp(lgb * c)

    def chunk(start, state, dmat, qdec, kdec, cdec, y_ref):
        q = q_ref[pl.ds(start, c), :]
        k = k_ref[pl.ds(start, c), :]
        v = v_ref[pl.ds(start, c), :]
        sc = _dot_nt(q, k) * dmat
        y = _dot(sc.astype(BF16), v) + _dot(q, state.astype(BF16)) * qdec
        y_ref[pl.ds(start, c), :] = y
        kdt = (k.astype(F32) * kdec).T.astype(BF16)
        return state * cdec + _dot(kdt, v)

    def body(t, carry):
        sf, sb = carry
        sf = chunk(pl.multiple_of(t * c, c), sf, dmat_f, qdec_f, kdec_f, cdec_f, yf_ref)
        sb = chunk(pl.multiple_of((n - 1 - t) * c, c), sb, dmat_b, qdec_b, kdec_b, cdec_b, yb_ref)
        return sf, sb

    if has_state:
        init = (s0f_ref[0, 0], s0b_ref[0, 0])
    else:
        init = (jnp.zeros((RET_DK, RET_DV), F32), jnp.zeros((RET_DK, RET_DV), F32))
    sf, sb = lax.fori_loop(0, n, body, init)
    sf_ref[0, 0] = sf
    sb_ref[0, 0] = sb

    y = yf_ref[...] + yb_ref[...]
    mu = jnp.mean(y, axis=-1, keepdims=True)
    yc = y - mu
    var = jnp.mean(yc * yc, axis=-1, keepdims=True)
    yn = yc * lax.rsqrt(var + EPS)
    o_ref[...] = (srg_ref[...] * (yn * gn_ref[...])).astype(BF16)


def _ret_call(rq, rk, rv, srg, gn, lf, lb, s0f, s0b, tok_off, batch, seq):
    off = tok_off // seq
    has_state = s0f is not None
    tokblk = lambda b, h: (off + b, h)
    st = lambda b, h: (b, h, 0, 0)
    in_specs = [
        pl.BlockSpec((seq, RET_DK), tokblk),
        pl.BlockSpec((seq, RET_DK), tokblk),
        pl.BlockSpec((seq, RET_DV), tokblk),
        pl.BlockSpec((seq, RET_DV), tokblk),
        pl.BlockSpec((1, RET_DV), lambda b, h: (0, h)),
        pl.BlockSpec((1, 1, LANE), lambda b, h: (h, 0, 0)),
        pl.BlockSpec((1, 1, LANE), lambda b, h: (h, 0, 0)),
    ]
    args = [rq, rk, rv, srg, gn, lf, lb]
    if has_state:
        in_specs += [pl.BlockSpec((1, 1, RET_DK, RET_DV), st)] * 2
        args += [s0f, s0b]
    st_shape = jax.ShapeDtypeStruct((batch, RET_HEADS, RET_DK, RET_DV), F32)
    return pl.pallas_call(
        functools.partial(_ret_kernel, has_state, seq),
        grid=(batch, RET_HEADS),
        in_specs=in_specs,
        out_specs=[
            pl.BlockSpec((seq, RET_DV), lambda b, h: (b, h)),
            pl.BlockSpec((1, 1, RET_DK, RET_DV), st),
            pl.BlockSpec((1, 1, RET_DK, RET_DV), st),
        ],
        out_shape=[jax.ShapeDtypeStruct((batch * seq, RET_HEADS * RET_DV), BF16), st_shape, st_shape],
        scratch_shapes=[pltpu.VMEM((seq, RET_DV), F32), pltpu.VMEM((seq, RET_DV), F32)],
        compiler_params=pltpu.CompilerParams(
            dimension_semantics=("parallel", "parallel"), vmem_limit_bytes=VMEM_LIMIT),
        name="ret_b%d" % batch,
    )(*args)


def _merge_kernel(np_blocks, ap_ref, as_ref, rp_ref, rs_ref, sga_ref, sgb_ref, x_ref, mod_ref, g_ref,
                  wa_ref, wb_ref, wo_ref, x1_ref, h2t_ref):
    is_prompt = pl.program_id(0) < np_blocks
    attn = jnp.where(is_prompt, ap_ref[...], as_ref[...])
    ret = jnp.where(is_prompt, rp_ref[...], rs_ref[...])
    m = sga_ref[...] * _dot(attn, wa_ref[...]) + sgb_ref[...] * _dot(ret, wb_ref[...])
    o = _dot(m.astype(BF16), wo_ref[...])
    x1 = x_ref[...] + mod_ref[0, 2:3, :] * o
    x1_ref[...] = x1
    h2 = _rms(x1, g_ref[...]) * (1.0 + mod_ref[0, 4:5, :]) + mod_ref[0, 3:4, :]
    h2t_ref[...] = h2.T.astype(BF16)


def _merge_call(attn_p, attn_s, ret_p, ret_s, sga, sgb, x, mod3, norm_ffn, wa, wb, wo, np_blocks, seq_blocks):
    n = x.shape[0]
    nblk = n // TM
    const = lambda i: (0, 0)
    row = lambda i: (i, 0)
    p_idx = lambda i: (jnp.minimum(i, np_blocks - 1), 0)
    s_idx = lambda i: (jnp.maximum(i - np_blocks, 0), 0)

    def mod_idx(i):
        return (jnp.where(i < np_blocks, 0, 1 + (i - np_blocks) // seq_blocks), 0, 0)

    w = attn_p.shape[1]
    return pl.pallas_call(
        functools.partial(_merge_kernel, np_blocks),
        grid=(nblk,),
        in_specs=[
            pl.BlockSpec((TM, w), p_idx),
            pl.BlockSpec((TM, w), s_idx),
            pl.BlockSpec((TM, w), p_idx),
            pl.BlockSpec((TM, w), s_idx),
            pl.BlockSpec((TM, D_MODEL), row),
            pl.BlockSpec((TM, D_MODEL), row),
            pl.BlockSpec((TM, D_MODEL), row),
            pl.BlockSpec((1, 8, D_MODEL), mod_idx),
            pl.BlockSpec((1, D_MODEL), const),
            pl.BlockSpec(wa.shape, const),
            pl.BlockSpec(wb.shape, const),
            pl.BlockSpec(wo.shape, const),
        ],
        out_specs=[pl.BlockSpec((TM, D_MODEL), row), pl.BlockSpec((D_MODEL, TM), lambda i: (0, i))],
        out_shape=[jax.ShapeDtypeStruct((n, D_MODEL), F32), jax.ShapeDtypeStruct((D_MODEL, n), BF16)],
        compiler_params=pltpu.CompilerParams(dimension_semantics=("parallel",), vmem_limit_bytes=VMEM_LIMIT),
        name="merge",
    )(attn_p, attn_s, ret_p, ret_s, sga, sgb, x, mod3, norm_ffn, wa, wb, wo)


def _top_rows(s, k, with_rank=False):
    rows = []
    rank = jnp.full(s.shape, float(k), F32) if with_rank else None
    for a in range(k):
        m = jnp.max(s, axis=0, keepdims=True)
        rows.append(m)
        eq = s == m
        if with_rank:
            rank = jnp.where(eq, float(a), rank)
        s = jnp.where(eq, -jnp.inf, s)
    return rows, rank


def _stack_rows(rows):
    n, cols = len(rows), rows[0].shape[1]
    ridx = lax.broadcasted_iota(jnp.int32, (n, cols), 0)
    out = jnp.broadcast_to(rows[0], (n, cols))
    for a in range(1, n):
        out = jnp.where(ridx == a, rows[a], out)
    return out


def _peer_route(qt_ref, k1_ref, k2_ref, r2_ref, e2_ref, cnt_ref, cc_ref):
    half = PEER_DQ // 2
    for hd in range(PEER_HEADS):
        kr = slice(hd * N_KEYS, (hd + 1) * N_KEYS)
        q1 = qt_ref[hd * PEER_DQ:hd * PEER_DQ + half, :].astype(BF16)
        q2 = qt_ref[hd * PEER_DQ + half:(hd + 1) * PEER_DQ, :].astype(BF16)
        s1 = _dot(k1_ref[kr, :], q1)
        s2 = _dot(k2_ref[kr, :], q2)
        r1, _ = _top_rows(s1, PEER_TOPK)
        r2, rank2 = _top_rows(s2, PEER_TOPK, with_rank=True)
        v1 = _stack_rows(r1)
        v2 = _stack_rows(r2)
        k8 = PEER_TOPK // 2
        cands = [r1[0] + v2, r1[1] + v2[0:k8, :], r1[2] + v2[0:k8, :], r1[3] + v2[0:k8, :],
                 v1 + r2[0], v1[0:k8, :] + r2[1], v1[0:k8, :] + r2[2]]
        tau = _top_rows(jnp.concatenate(cands, axis=0), PEER_TOPK)[0][-1]
        e1 = [jnp.exp(r - r1[0]) for r in r1]
        e2v = jnp.exp(v2 - r2[0])
        cnts, zs = [], []
        for a in range(PEER_TOPK):
            sel = (r1[a] + v2) >= tau
            cnts.append(jnp.sum(jnp.where(sel, 1.0, 0.0), axis=0, keepdims=True))
            zs.append(jnp.sum(jnp.where(sel, e2v, 0.0), axis=0, keepdims=True))
        z = e1[0] * zs[0]
        for a in range(1, PEER_TOPK):
            z = z + e1[a] * zs[a]
        zinv = 1.0 / z
        cnt = jnp.zeros(s1.shape, F32)
        cc = jnp.zeros(s1.shape, F32)
        for a in range(PEER_TOPK):
            eq = s1 == r1[a]
            cnt = jnp.where(eq, cnts[a], cnt)
            cc = jnp.where(eq, e1[a] * zinv, cc)
        r2_ref[kr, :] = rank2.astype(BF16)
        e2_ref[kr, :] = jnp.exp(s2 - r2[0]).astype(BF16)
        cnt_ref[kr, :] = cnt
        cc_ref[kr, :] = cc


def _peer_kernel(h2t_ref, x1_ref, mod_ref, g_ref, wqt_ref, k1_ref, k2_ref, u_ref, vt_ref,
                 y_ref, qt_ref, r2_ref, e2_ref, cnt_ref, cc_ref, acc_ref):
    j = pl.program_id(1)
    nj = pl.num_programs(1)
    tm = h2t_ref.shape[1]

    @pl.when(j == 0)
    def _():
        qt_ref[...] = _dot(wqt_ref[...], h2t_ref[...])
        _peer_route(qt_ref, k1_ref, k2_ref, r2_ref, e2_ref, cnt_ref, cc_ref)
        acc_ref[...] = jnp.zeros(acc_ref.shape, F32)

    hu = _dot(u_ref[...], h2t_ref[...])
    act = jax.nn.gelu(hu, approximate=True).astype(BF16)
    parts = []
    for il in range(E_CHUNK // N_KEYS):
        g = None
        for hd in range(PEER_HEADS):
            r = hd * N_KEYS + j * (E_CHUNK // N_KEYS) + il
            cnt = jnp.broadcast_to(cnt_ref[pl.ds(r, 1), :], (N_KEYS, tm)).astype(BF16)
            cc = jnp.broadcast_to(cc_ref[pl.ds(r, 1), :], (N_KEYS, tm)).astype(BF16)
            kr = slice(hd * N_KEYS, (hd + 1) * N_KEYS)
            e2 = e2_ref[kr, :]
            t = jnp.where(r2_ref[kr, :] < cnt, e2, jnp.zeros_like(e2)) * cc
            g = t if g is None else g + t
        parts.append(g * act[il * N_KEYS:(il + 1) * N_KEYS, :])
    p = jnp.concatenate(parts, axis=0)
    acc_ref[...] += _dot(vt_ref[...], p)

    @pl.when(j == nj - 1)
    def _():
        x2 = x1_ref[...] + mod_ref[0, 5:6, :] * acc_ref[...].T
        y_ref[...] = _rms(x2, g_ref[...])


def _peer_call(h2t, x1, mod3, norm_final, wqt, k1, k2, u_b, vt_b, np_tokens, seq):
    n = x1.shape[0]
    tm = TM_PEER
    nblk = n // tm
    npb = np_tokens // tm
    sb = seq // tm
    ne = u_b.shape[0] // E_CHUNK
    nq = PEER_HEADS * PEER_DQ
    nk = PEER_HEADS * N_KEYS

    def mod_idx(i, j):
        return (jnp.where(i < npb, 0, 1 + (i - npb) // sb), 0, 0)

    const = lambda i, j: (0, 0)
    return pl.pallas_call(
        _peer_kernel,
        grid=(nblk, ne),
        in_specs=[
            pl.BlockSpec((D_MODEL, tm), lambda i, j: (0, i)),
            pl.BlockSpec((tm, D_MODEL), lambda i, j: (i, 0)),
            pl.BlockSpec((1, 8, D_MODEL), mod_idx),
            pl.BlockSpec((1, D_MODEL), const),
            pl.BlockSpec(wqt.shape, const),
            pl.BlockSpec(k1.shape, const),
            pl.BlockSpec(k2.shape, const),
            pl.BlockSpec((E_CHUNK, D_MODEL), lambda i, j: (j, 0)),
            pl.BlockSpec((D_MODEL, E_CHUNK), lambda i, j: (0, j)),
        ],
        out_specs=pl.BlockSpec((tm, D_MODEL), lambda i, j: (i, 0)),
        out_shape=jax.ShapeDtypeStruct((n, D_MODEL), F32),
        scratch_shapes=[
            pltpu.VMEM((nq, tm), F32),
            pltpu.VMEM((nk, tm), BF16),
            pltpu.VMEM((nk, tm), BF16),
            pltpu.VMEM((nk, tm), F32),
            pltpu.VMEM((nk, tm), F32),
            pltpu.VMEM((D_MODEL, tm), F32),
        ],
        compiler_params=pltpu.CompilerParams(
            dimension_semantics=("parallel", "arbitrary"), vmem_limit_bytes=VMEM_LIMIT),
        name="peer",
    )(h2t, x1, mod3, norm_final, wqt, k1, k2, u_b, vt_b)


def _rope_tables(seq):
    rows = seq // GRID_W
    row = jnp.repeat(jnp.arange(rows, dtype=F32), GRID_W)
    col = jnp.tile(jnp.arange(GRID_W, dtype=F32), rows)
    nf = MLA_ROPE // 4
    freqs = jnp.power(ROPE_BASE, -jnp.arange(nf, dtype=F32) / nf)
    ang = jnp.concatenate([row[:, None] * freqs, col[:, None] * freqs], axis=-1)
    cos, sin = jnp.cos(ang), jnp.sin(ang)
    pad = jnp.zeros((seq, LANE - MLA_ROPE), F32)
    ctab = jnp.concatenate([cos, cos, pad], axis=1)
    stab = jnp.concatenate([-sin, sin, pad], axis=1)
    cid = jnp.concatenate([jnp.ones((TM, MLA_ROPE), F32), jnp.zeros((TM, LANE - MLA_ROPE), F32)], axis=1)
    ctab = jnp.concatenate([ctab, cid], axis=0)
    stab = jnp.concatenate([stab, jnp.zeros((TM, LANE), F32)], axis=0)
    return ctab, stab


def _pad_cols(a, width):
    return jnp.concatenate([a, jnp.zeros((a.shape[0], width - a.shape[1]), a.dtype)], axis=1)


def kernel(x_prompt, x_sample, c, cache_ckv, cache_krope, state_ret_fwd, state_ret_bwd, c_ctx, w_mod, b_mod, norm_mix, norm_ffn, norm_final, w_in, q_norm, kv_norm, w_uq, w_ukv, ret_logit_fwd, ret_logit_bwd, ret_gn, w_up_a, w_up_b, w_o, peer_wq, peer_keys1, peer_keys2, peer_u, peer_v):
    depth = w_mod.shape[0]
    assert depth == 1
    bp, sp, _ = x_prompt.shape
    bs, ss, _ = x_sample.shape
    past = cache_ckv.shape[2]
    n_p, n_s = bp * sp, bs * ss
    assert 1 + bs <= MOD_ROWS and sp == TM and ss % TM == 0 and ss % TM_PEER == 0 and n_p % TM_PEER == 0
    np_blocks, seq_blocks = n_p // TM, ss // TM
    l = 0

    half = MLA_ROPE // 2
    kr1 = w_in[l][:, 384:384 + half]
    kr2 = w_in[l][:, 384 + half:384 + MLA_ROPE]
    win_p = jnp.concatenate([
        w_in[l][:, :384],
        _pad_cols(jnp.concatenate([kr1, kr2], axis=1), LANE),
        _pad_cols(jnp.concatenate([kr2, kr1], axis=1), LANE),
        w_in[l][:, 384 + MLA_ROPE:],
    ], axis=1).astype(BF16)
    assert win_p.shape[1] == _D_IN_P
    w3 = w_uq[l].reshape(Q_LORA, MLA_HEADS, MLA_NOPE + MLA_ROPE)
    r1 = w3[:, :, MLA_NOPE:MLA_NOPE + half]
    r2 = w3[:, :, MLA_NOPE + half:]
    zpad = jnp.zeros((Q_LORA, MLA_HEADS, LANE - MLA_ROPE), F32)
    wuq_all = jnp.concatenate([
        w3[:, :, :MLA_NOPE].reshape(Q_LORA, MLA_HEADS * MLA_NOPE),
        jnp.concatenate([r1, r2, zpad], axis=2).reshape(Q_LORA, MLA_HEADS * LANE),
        jnp.concatenate([r2, r1, zpad], axis=2).reshape(Q_LORA, MLA_HEADS * LANE),
    ], axis=1).astype(BF16)
    wkv3 = w_ukv[l].reshape(KV_LORA, MLA_HEADS, MLA_NOPE + MLA_V)
    eye = jnp.eye(MLA_HEADS, dtype=F32)
    wk_hdl = jnp.transpose(wkv3[:, :, :MLA_NOPE], (1, 2, 0))
    wv_hld = jnp.transpose(wkv3[:, :, MLA_NOPE:], (1, 0, 2))
    bdk = (wk_hdl[:, :, None, :] * eye[:, None, :, None]).reshape(
        MLA_HEADS * MLA_NOPE, MLA_HEADS * KV_LORA).astype(BF16)
    bdv = (wv_hld[:, :, None, :] * eye[:, None, :, None]).reshape(
        MLA_HEADS * KV_LORA, MLA_HEADS * MLA_V).astype(BF16)
    wa = w_up_a[l].astype(BF16)
    wb = w_up_b[l].astype(BF16)
    wo = w_o[l].astype(BF16)
    wqt = peer_wq[l].T.astype(BF16)
    k1 = peer_keys1[l].reshape(PEER_HEADS * N_KEYS, PEER_DQ // 2).astype(BF16)
    k2 = peer_keys2[l].reshape(PEER_HEADS * N_KEYS, PEER_DQ // 2).astype(BF16)
    u_b = peer_u[l].astype(BF16)
    vt_b = peer_v[l].T.astype(BF16)
    ctab, stab = _rope_tables(ss)
    lf = jnp.broadcast_to(ret_logit_fwd[l][:, None, None], (RET_HEADS, 1, LANE))
    lb = jnp.broadcast_to(ret_logit_bwd[l][:, None, None], (RET_HEADS, 1, LANE))

    cc = jnp.concatenate([c_ctx[None, :], c, jnp.zeros((MOD_ROWS - 1 - bs, D_MODEL), F32)], axis=0)
    mod = _mod_call(cc, w_mod[l], b_mod[l][None, :])
    mod3 = jnp.concatenate([mod.reshape(MOD_ROWS, 6, D_MODEL), jnp.zeros((MOD_ROWS, 2, D_MODEL), F32)], axis=1)

    x = jnp.concatenate([x_prompt.reshape(n_p, D_MODEL), x_sample.reshape(n_s, D_MODEL)], axis=0)
    qt, kk, ckvt, ckv32, kr32, rq, rk, rv, srg, sga, sgb = _proj_call(
        x, mod3, norm_mix[l][None, :], win_p, q_norm[l][None, :], kv_norm[l][None, :], wuq_all.T, bdk.T,
        ctab, stab, np_blocks, seq_blocks)

    kk_p = kk[:n_p].reshape(bp, sp, 2 * LANE)
    vt_p = jnp.transpose(ckvt[:, :n_p].reshape(KV_LORA, bp, sp), (1, 0, 2))
    cache_kk = jnp.concatenate([cache_ckv[:, l], _pad_cols(
        cache_krope[:, l].reshape(bs * past, MLA_ROPE), LANE).reshape(bs, past, LANE)], axis=2).astype(BF16)
    kk_s = jnp.concatenate([kk[n_p:].reshape(bs, ss, 2 * LANE), cache_kk], axis=1)
    vt_s = jnp.concatenate([jnp.transpose(ckvt[:, n_p:].reshape(KV_LORA, bs, ss), (1, 0, 2)),
                            jnp.transpose(cache_ckv[:, l], (0, 2, 1)).astype(BF16)], axis=2)
    attn_p = _attn_call(qt, kk_p, vt_p, bdv, 0, bp, sp, sp)
    attn_s = _attn_call(qt, kk_s, vt_s, bdv, n_p, bs, ss, 512)

    gn = ret_gn[l][None, :]
    ret_p, sf, sb = _ret_call(rq, rk, rv, srg, gn, lf, lb, None, None, 0, bp, sp)
    ret_s, _, _ = _ret_call(rq, rk, rv, srg, gn, lf, lb, state_ret_fwd[:, l], state_ret_bwd[:, l], n_p, bs, ss)

    x1, h2t = _merge_call(attn_p, attn_s, ret_p, ret_s, sga, sgb, x, mod3, norm_ffn[l][None, :],
                          wa, wb, wo, np_blocks, seq_blocks)

    y = _peer_call(h2t, x1, mod3, norm_final[None, :], wqt, k1, k2, u_b, vt_b, n_p, ss)

    y_prompt = y[:n_p].reshape(bp, sp, D_MODEL)
    y_sample = y[n_p:].reshape(bs, ss, D_MODEL)
    new_ckv = ckv32[:n_p].reshape(bp, 1, sp, KV_LORA)
    new_kr = kr32[:n_p, :MLA_ROPE].reshape(bp, 1, sp, MLA_ROPE)
    return (y_prompt, y_sample, new_ckv, new_kr, sf[:, None], sb[:, None])
```

```python
import functools

import jax
import jax.numpy as jnp
from jax import lax
from jax.experimental import pallas as pl
from jax.experimental.pallas import tpu as pltpu

F32 = jnp.float32
BF16 = jnp.bfloat16

D_MODEL = 1024
GRID_W = 64
EPS = 1e-6
MLA_HEADS = 8
MLA_NOPE = 64
MLA_ROPE = 32
MLA_V = 64
Q_LORA = 256
KV_LORA = 128
ROPE_BASE = 10000.0
RET_HEADS = 4
RET_DK = 128
RET_DV = 128
PEER_HEADS = 8
PEER_DQ = 256
N_KEYS = 128
PEER_TOPK = 16

LANE = 128
VMEM_LIMIT = 56 * 1024 * 1024

TM = 256
TQ = 128
RET_CHUNK = 256
TM_PEER = 512
E_CHUNK = 1024
MOD_ROWS = 16

_C_CQ = 0
_C_CKV = 256
_C_KR = 384
_C_KRS = 512
_C_RQ = 640
_C_RK = 1152
_C_RV = 1664
_C_RG = 2176
_C_GA = 2688
_C_GB = 3712
_D_IN_P = 4736


def _rms(x, g):
    return x * lax.rsqrt(jnp.mean(x * x, axis=-1, keepdims=True) + EPS) * g


def _dot(a, b):
    return jnp.dot(a, b, preferred_element_type=F32)


def _dot_nt(a, b):
    return lax.dot_general(a, b, (((1,), (1,)), ((), ())), preferred_element_type=F32)


def _mod_kernel(c_ref, w_ref, b_ref, o_ref):
    c = c_ref[...]
    s = c * jax.nn.sigmoid(c)
    o_ref[...] = _dot(s.astype(BF16), w_ref[...].astype(BF16)) + b_ref[...]


def _mod_call(cc, w_mod, b_mod):
    n = w_mod.shape[1]
    bn = 1024
    return pl.pallas_call(
        _mod_kernel,
        grid=(n // bn,),
        in_specs=[
            pl.BlockSpec((MOD_ROWS, D_MODEL), lambda j: (0, 0)),
            pl.BlockSpec((D_MODEL, bn), lambda j: (0, j)),
            pl.BlockSpec((1, bn), lambda j: (0, j)),
        ],
        out_specs=pl.BlockSpec((MOD_ROWS, bn), lambda j: (0, j)),
        out_shape=jax.ShapeDtypeStruct((MOD_ROWS, n), F32),
        compiler_params=pltpu.CompilerParams(dimension_semantics=("parallel",), vmem_limit_bytes=VMEM_LIMIT),
        name="mod",
    )(cc, w_mod, b_mod)


def _proj_kernel(np_blocks, xp_ref, xs_ref, mod_ref, g_ref, win_ref, qn_ref, kvn_ref, wuqt_ref, bdkt_ref,
                 ct_ref, st_ref, ctt_ref, stt_ref,
                 qt_ref, kk_ref, ckvt_ref, ckv_ref, kr_ref, rq_ref, rk_ref, rv_ref, srg_ref, sga_ref, sgb_ref):
    x = jnp.where(pl.program_id(0) < np_blocks, xp_ref[...], xs_ref[...])
    h = _rms(x, g_ref[...]) * (1.0 + mod_ref[0, 1:2, :]) + mod_ref[0, 0:1, :]
    y = _dot(h.astype(BF16), win_ref[...])

    ct = ct_ref[...]
    st = st_ref[...]
    ckv = _rms(y[:, _C_CKV:_C_CKV + KV_LORA], kvn_ref[...])
    kr = y[:, _C_KR:_C_KR + LANE]
    krot = kr * ct + y[:, _C_KRS:_C_KRS + LANE] * st
    ckv_ref[...] = ckv
    kr_ref[...] = kr
    kk_ref[:, 0:KV_LORA] = ckv.astype(BF16)
    kk_ref[:, KV_LORA:2 * KV_LORA] = krot.astype(BF16)
    ckvt_ref[...] = ckv.T.astype(BF16)

    cq = _rms(y[:, _C_CQ:_C_CQ + Q_LORA], qn_ref[...])
    qat = _dot(wuqt_ref[...], cq.T.astype(BF16))
    nn = MLA_HEADS * MLA_NOPE
    nr = MLA_HEADS * LANE
    qlat = _dot(bdkt_ref[...], qat[:nn, :].astype(BF16))
    ctt = ctt_ref[...]
    stt = stt_ref[...]
    scale = (MLA_NOPE + MLA_ROPE) ** -0.5
    for hd in range(MLA_HEADS):
        lo = hd * LANE
        qrot = qat[nn + lo:nn + lo + LANE, :] * ctt + qat[nn + nr + lo:nn + nr + lo + LANE, :] * stt
        qt_ref[2 * lo:2 * lo + LANE, :] = (qlat[lo:lo + LANE, :] * scale).astype(BF16)
        qt_ref[2 * lo + LANE:2 * lo + 2 * LANE, :] = (qrot * scale).astype(BF16)

    hk = RET_HEADS * RET_DK
    rq_ref[...] = y[:, _C_RQ:_C_RQ + hk].astype(BF16)
    rk_ref[...] = (y[:, _C_RK:_C_RK + hk] * (RET_DK ** -0.5)).astype(BF16)
    rv_ref[...] = y[:, _C_RV:_C_RV + hk].astype(BF16)
    rg = y[:, _C_RG:_C_RG + hk]
    srg_ref[...] = rg * jax.nn.sigmoid(rg)
    sga_ref[...] = jax.nn.sigmoid(y[:, _C_GA:_C_GA + D_MODEL])
    sgb_ref[...] = jax.nn.sigmoid(y[:, _C_GB:_C_GB + D_MODEL])


def _proj_call(xp, xs, mod3, norm_mix, win_p, q_norm, kv_norm, wuq_t, bdk_t, ctab, stab, np_blocks, seq_blocks):
    n = xp.shape[0] + xs.shape[0]
    nblk = n // TM
    rope_id_block = ctab.shape[0] // TM - 1

    def mod_idx(i):
        return (jnp.where(i < np_blocks, 0, 1 + (i - np_blocks) // seq_blocks), 0, 0)

    def rope_blk(i):
        return jnp.where(i < np_blocks, rope_id_block, (i - np_blocks) % seq_blocks)

    rope_idx = lambda i: (rope_blk(i), 0)
    rope_idx_t = lambda i: (0, rope_blk(i))
    const = lambda i: (0, 0)
    row = lambda i: (i, 0)
    col = lambda i: (0, i)
    hk = RET_HEADS * RET_DK
    out_shape = [
        jax.ShapeDtypeStruct((2 * MLA_HEADS * LANE, n), BF16),
        jax.ShapeDtypeStruct((n, 2 * LANE), BF16),
        jax.ShapeDtypeStruct((KV_LORA, n), BF16),
        jax.ShapeDtypeStruct((n, KV_LORA), F32),
        jax.ShapeDtypeStruct((n, LANE), F32),
        jax.ShapeDtypeStruct((n, hk), BF16),
        jax.ShapeDtypeStruct((n, hk), BF16),
        jax.ShapeDtypeStruct((n, hk), BF16),
        jax.ShapeDtypeStruct((n, hk), F32),
        jax.ShapeDtypeStruct((n, D_MODEL), F32),
        jax.ShapeDtypeStruct((n, D_MODEL), F32),
    ]
    out_specs = [pl.BlockSpec((s.shape[0], TM), col) if s.shape[1] == n else pl.BlockSpec((TM, s.shape[1]), row)
                 for s in out_shape]
    return pl.pallas_call(
        functools.partial(_proj_kernel, np_blocks),
        grid=(nblk,),
        in_specs=[
            pl.BlockSpec((TM, D_MODEL), lambda i: (jnp.minimum(i, np_blocks - 1), 0)),
            pl.BlockSpec((TM, D_MODEL), lambda i: (jnp.maximum(i - np_blocks, 0), 0)),
            pl.BlockSpec((1, 8, D_MODEL), mod_idx),
            pl.BlockSpec((1, D_MODEL), const),
            pl.BlockSpec(win_p.shape, const),
            pl.BlockSpec((1, Q_LORA), const),
            pl.BlockSpec((1, KV_LORA), const),
            pl.BlockSpec(wuq_t.shape, const),
            pl.BlockSpec(bdk_t.shape, const),
            pl.BlockSpec((TM, LANE), rope_idx),
            pl.BlockSpec((TM, LANE), rope_idx),
            pl.BlockSpec((LANE, TM), rope_idx_t),
            pl.BlockSpec((LANE, TM), rope_idx_t),
        ],
        out_specs=out_specs,
        out_shape=out_shape,
        compiler_params=pltpu.CompilerParams(dimension_semantics=("parallel",), vmem_limit_bytes=VMEM_LIMIT),
        name="proj",
    )(xp, xs, mod3, norm_mix, win_p, q_norm, kv_norm, wuq_t, bdk_t, ctab, stab, ctab.T, stab.T)


def _attn_kernel(nkb, tkb, qt_ref, k_ref, vt_ref, bdv_ref, o_ref, s_ref):
    fw = 2 * LANE
    qt = jnp.concatenate([qt_ref[hd * fw:(hd + 1) * fw, :] for hd in range(MLA_HEADS)], axis=1)
    m = None
    for kb in range(nkb):
        s = _dot(k_ref[0, kb * tkb:(kb + 1) * tkb, :], qt)
        s_ref[kb * tkb:(kb + 1) * tkb, :] = s
        mk = jnp.max(s, axis=0, keepdims=True)
        m = mk if m is None else jnp.maximum(m, mk)
    l = None
    acc = None
    for kb in range(nkb):
        p = jnp.exp(s_ref[kb * tkb:(kb + 1) * tkb, :] - m)
        lk = jnp.sum(p, axis=0, keepdims=True)
        ak = _dot(vt_ref[0, :, kb * tkb:(kb + 1) * tkb], p.astype(BF16))
        l = lk if l is None else l + lk
        acc = ak if acc is None else acc + ak
    o = acc / l
    ocat = jnp.concatenate([o[:, hd * TQ:(hd + 1) * TQ].T for hd in range(MLA_HEADS)], axis=1)
    o_ref[...] = _dot(ocat.astype(BF16), bdv_ref[...]).astype(BF16)


def _attn_call(qt, kk3, vt3, bdv, tok_off, batch, seq, tkb):
    nq = seq // TQ
    tk = kk3.shape[1]
    off = tok_off // TQ
    return pl.pallas_call(
        functools.partial(_attn_kernel, tk // tkb, tkb),
        grid=(batch, nq),
        in_specs=[
            pl.BlockSpec((qt.shape[0], TQ), lambda b, qi: (0, off + b * nq + qi)),
            pl.BlockSpec((1, tk, 2 * LANE), lambda b, qi: (b, 0, 0)),
            pl.BlockSpec((1, KV_LORA, tk), lambda b, qi: (b, 0, 0)),
            pl.BlockSpec(bdv.shape, lambda b, qi: (0, 0)),
        ],
        out_specs=pl.BlockSpec((TQ, MLA_HEADS * MLA_V), lambda b, qi: (b * nq + qi, 0)),
        out_shape=jax.ShapeDtypeStruct((batch * seq, MLA_HEADS * MLA_V), BF16),
        scratch_shapes=[pltpu.VMEM((tk, MLA_HEADS * TQ), F32)],
        compiler_params=pltpu.CompilerParams(
            dimension_semantics=("parallel", "parallel"), vmem_limit_bytes=VMEM_LIMIT),
        name="attn_b%d" % batch,
    )(qt, kk3, vt3, bdv)


def _log_sigmoid(x):
    return jnp.minimum(x, 0.0) - jnp.log(1.0 + jnp.exp(-jnp.abs(x)))


def _ret_kernel(has_state, seq, *refs):
    if has_state:
        (q_ref, k_ref, v_ref, srg_ref, gn_ref, lf_ref, lb_ref, s0f_ref, s0b_ref,
         o_ref, sf_ref, sb_ref, yf_ref, yb_ref) = refs
    else:
        (q_ref, k_ref, v_ref, srg_ref, gn_ref, lf_ref, lb_ref,
         o_ref, sf_ref, sb_ref, yf_ref, yb_ref) = refs
    c = RET_CHUNK
    n = seq // c
    lgf = _log_sigmoid(lf_ref[0, :, 0:1])
    lgb = _log_sigmoid(lb_ref[0, :, 0:1])
    ii = lax.broadcasted_iota(jnp.int32, (c, c), 0)
    jj = lax.broadcasted_iota(jnp.int32, (c, c), 1)
    diff = (ii - jj).astype(F32)
    dmat_f = jnp.where(diff >= 0, jnp.exp(lgf * jnp.maximum(diff, 0.0)), 0.0)
    dmat_b = jnp.where(diff <= 0, jnp.exp(lgb * jnp.maximum(-diff, 0.0)), 0.0)
    idx = lax.broadcasted_iota(jnp.int32, (c, 1), 0).astype(F32)
    qdec_f = jnp.exp(lgf * (idx + 1.0))
    kdec_f = jnp.exp(lgf * (c - 1.0 - idx))
    cdec_f = jnp.exp(lgf * c)
    qdec_b = jnp.exp(lgb * (c - idx))
    kdec_b = jnp.exp(lgb * idx)
    cdec_b = jnp.exp(lgb * c)

    def chunk(start, state, dmat, qdec, kdec, cdec, y_ref):
        q = q_ref[pl.ds(start, c), :]
        k = k_ref[pl.ds(start, c), :]
        v = v_ref[pl.ds(start, c), :]
        sc = _dot_nt(q, k) * dmat
        y = _dot(sc.astype(BF16), v) + _dot(q, state.astype(BF16)) * qdec
        y_ref[pl.ds(start, c), :] = y
        kdt = (k.astype(F32) * kdec).T.astype(BF16)
        return state * cdec + _dot(kdt, v)

    def body(t, carry):
        sf, sb = carry
        sf = chunk(pl.multiple_of(t * c, c), sf, dmat_f, qdec_f, kdec_f, cdec_f, yf_ref)
        sb = chunk(pl.multiple_of((n - 1 - t) * c, c), sb, dmat_b, qdec_b, kdec_b, cdec_b, yb_ref)
        return sf, sb

    if has_state:
        init = (s0f_ref[0, 0], s0b_ref[0, 0])
    else:
        init = (jnp.zeros((RET_DK, RET_DV), F32), jnp.zeros((RET_DK, RET_DV), F32))
    sf, sb = lax.fori_loop(0, n, body, init)
    sf_ref[0, 0] = sf
    sb_ref[0, 0] = sb

    y = yf_ref[...] + yb_ref[...]
    mu = jnp.mean(y, axis=-1, keepdims=True)
    yc = y - mu
    var = jnp.mean(yc * yc, axis=-1, keepdims=True)
    yn = yc * lax.rsqrt(var + EPS)
    o_ref[...] = (srg_ref[...] * (yn * gn_ref[...])).astype(BF16)


def _ret_call(rq, rk, rv, srg, gn, lf, lb, s0f, s0b, tok_off, batch, seq):
    off = tok_off // seq
    has_state = s0f is not None
    tokblk = lambda b, h: (off + b, h)
    st = lambda b, h: (b, h, 0, 0)
    in_specs = [
        pl.BlockSpec((seq, RET_DK), tokblk),
        pl.BlockSpec((seq, RET_DK), tokblk),
        pl.BlockSpec((seq, RET_DV), tokblk),
        pl.BlockSpec((seq, RET_DV), tokblk),
        pl.BlockSpec((1, RET_DV), lambda b, h: (0, h)),
        pl.BlockSpec((1, 1, LANE), lambda b, h: (h, 0, 0)),
        pl.BlockSpec((1, 1, LANE), lambda b, h: (h, 0, 0)),
    ]
    args = [rq, rk, rv, srg, gn, lf, lb]
    if has_state:
        in_specs += [pl.BlockSpec((1, 1, RET_DK, RET_DV), st)] * 2
        args += [s0f, s0b]
    st_shape = jax.ShapeDtypeStruct((batch, RET_HEADS, RET_DK, RET_DV), F32)
    return pl.pallas_call(
        functools.partial(_ret_kernel, has_state, seq),
        grid=(batch, RET_HEADS),
        in_specs=in_specs,
        out_specs=[
            pl.BlockSpec((seq, RET_DV), lambda b, h: (b, h)),
            pl.BlockSpec((1, 1, RET_DK, RET_DV), st),
            pl.BlockSpec((1, 1, RET_DK, RET_DV), st),
        ],
        out_shape=[jax.ShapeDtypeStruct((batch * seq, RET_HEADS * RET_DV), BF16), st_shape, st_shape],
        scratch_shapes=[pltpu.VMEM((seq, RET_DV), F32), pltpu.VMEM((seq, RET_DV), F32)],
        compiler_params=pltpu.CompilerParams(
            dimension_semantics=("parallel", "parallel"), vmem_limit_bytes=VMEM_LIMIT),
        name="ret_b%d" % batch,
    )(*args)


def _merge_kernel(np_blocks, ap_ref, as_ref, rp_ref, rs_ref, sga_ref, sgb_ref, xp_ref, xs_ref, mod_ref, g_ref,
                  wa_ref, wb_ref, wo_ref, x1_ref, h2_ref):
    is_prompt = pl.program_id(0) < np_blocks
    attn = jnp.where(is_prompt, ap_ref[...], as_ref[...])
    ret = jnp.where(is_prompt, rp_ref[...], rs_ref[...])
    x = jnp.where(is_prompt, xp_ref[...], xs_ref[...])
    m = sga_ref[...] * _dot(attn, wa_ref[...]) + sgb_ref[...] * _dot(ret, wb_ref[...])
    o = _dot(m.astype(BF16), wo_ref[...])
    x1 = x + mod_ref[0, 2:3, :] * o
    x1_ref[...] = x1
    h2 = _rms(x1, g_ref[...]) * (1.0 + mod_ref[0, 4:5, :]) + mod_ref[0, 3:4, :]
    h2_ref[...] = h2.astype(BF16)


def _merge_call(attn_p, attn_s, ret_p, ret_s, sga, sgb, xp, xs, mod3, norm_ffn, wa, wb, wo, np_blocks, seq_blocks):
    n = xp.shape[0] + xs.shape[0]
    nblk = n // TM
    const = lambda i: (0, 0)
    row = lambda i: (i, 0)
    p_idx = lambda i: (jnp.minimum(i, np_blocks - 1), 0)
    s_idx = lambda i: (jnp.maximum(i - np_blocks, 0), 0)

    def mod_idx(i):
        return (jnp.where(i < np_blocks, 0, 1 + (i - np_blocks) // seq_blocks), 0, 0)

    w = attn_p.shape[1]
    return pl.pallas_call(
        functools.partial(_merge_kernel, np_blocks),
        grid=(nblk,),
        in_specs=[
            pl.BlockSpec((TM, w), p_idx),
            pl.BlockSpec((TM, w), s_idx),
            pl.BlockSpec((TM, w), p_idx),
            pl.BlockSpec((TM, w), s_idx),
            pl.BlockSpec((TM, D_MODEL), row),
            pl.BlockSpec((TM, D_MODEL), row),
            pl.BlockSpec((TM, D_MODEL), p_idx),
            pl.BlockSpec((TM, D_MODEL), s_idx),
            pl.BlockSpec((1, 8, D_MODEL), mod_idx),
            pl.BlockSpec((1, D_MODEL), const),
            pl.BlockSpec(wa.shape, const),
            pl.BlockSpec(wb.shape, const),
            pl.BlockSpec(wo.shape, const),
        ],
        out_specs=[pl.BlockSpec((TM, D_MODEL), row), pl.BlockSpec((TM, D_MODEL), row)],
        out_shape=[jax.ShapeDtypeStruct((n, D_MODEL), F32), jax.ShapeDtypeStruct((n, D_MODEL), BF16)],
        compiler_params=pltpu.CompilerParams(dimension_semantics=("parallel",), vmem_limit_bytes=VMEM_LIMIT),
        name="merge",
    )(attn_p, attn_s, ret_p, ret_s, sga, sgb, xp, xs, mod3, norm_ffn, wa, wb, wo)


def _top_rows(s, k, with_rank=False):
    rows = []
    rank = jnp.full(s.shape, float(k), F32) if with_rank else None
    for a in range(k):
        m = jnp.max(s, axis=0, keepdims=True)
        rows.append(m)
        eq = s == m
        if with_rank:
            rank = jnp.where(eq, float(a), rank)
        s = jnp.where(eq, -jnp.inf, s)
    return rows, rank


def _stack_rows(rows):
    n, cols = len(rows), rows[0].shape[1]
    ridx = lax.broadcasted_iota(jnp.int32, (n, cols), 0)
    out = jnp.broadcast_to(rows[0], (n, cols))
    for a in range(1, n):
        out = jnp.where(ridx == a, rows[a], out)
    return out


def _peer_route(qt_ref, k1_ref, k2_ref, r2_ref, e2_ref, cnt_ref, cc_ref):
    half = PEER_DQ // 2
    for hd in range(PEER_HEADS):
        kr = slice(hd * N_KEYS, (hd + 1) * N_KEYS)
        q1 = qt_ref[hd * PEER_DQ:hd * PEER_DQ + half, :].astype(BF16)
        q2 = qt_ref[hd * PEER_DQ + half:(hd + 1) * PEER_DQ, :].astype(BF16)
        s1 = _dot(k1_ref[kr, :], q1)
        s2 = _dot(k2_ref[kr, :], q2)
        r1, _ = _top_rows(s1, PEER_TOPK)
        r2, rank2 = _top_rows(s2, PEER_TOPK, with_rank=True)
        v1 = _stack_rows(r1)
        v2 = _stack_rows(r2)
        k8 = PEER_TOPK // 2
        cands = [r1[0] + v2, r1[1] + v2[0:k8, :], r1[2] + v2[0:k8, :], r1[3] + v2[0:k8, :],
                 v1 + r2[0], v1[0:k8, :] + r2[1], v1[0:k8, :] + r2[2]]
        tau = _top_rows(jnp.concatenate(cands, axis=0), PEER_TOPK)[0][-1]
        e1 = [jnp.exp(r - r1[0]) for r in r1]
        e2v = jnp.exp(v2 - r2[0])
        cnts, zs = [], []
        for a in range(PEER_TOPK):
            sel = (r1[a] + v2) >= tau
            cnts.append(jnp.sum(jnp.where(sel, 1.0, 0.0), axis=0, keepdims=True))
            zs.append(jnp.sum(jnp.where(sel, e2v, 0.0), axis=0, keepdims=True))
        z = e1[0] * zs[0]
        for a in range(1, PEER_TOPK):
            z = z + e1[a] * zs[a]
        zinv = 1.0 / z
        cnt = jnp.zeros(s1.shape, F32)
        for a in range(PEER_TOPK):
            cnt = jnp.where(s1 == r1[a], cnts[a], cnt)
        r2_ref[kr, :] = rank2.astype(BF16)
        e2_ref[kr, :] = jnp.exp(s2 - r2[0]).astype(BF16)
        cnt_ref[kr, :] = cnt
        cc_ref[kr, :] = jnp.exp(s1 - r1[0]) * zinv


def _peer_kernel(h2_ref, x1_ref, mod_ref, g_ref, wqt_ref, k1_ref, k2_ref, u_ref, vt_ref,
                 y_ref, qt_ref, r2_ref, e2_ref, cnt_ref, cc_ref, acc_ref):
    j = pl.program_id(1)
    nj = pl.num_programs(1)
    tm = h2_ref.shape[0]

    @pl.when(j == 0)
    def _():
        qt_ref[...] = _dot_nt(wqt_ref[...], h2_ref[...])
        _peer_route(qt_ref, k1_ref, k2_ref, r2_ref, e2_ref, cnt_ref, cc_ref)
        acc_ref[...] = jnp.zeros(acc_ref.shape, F32)

    hu = _dot_nt(u_ref[...], h2_ref[...])
    act = jax.nn.gelu(hu.astype(BF16), approximate=True)
    parts = []
    for il in range(E_CHUNK // N_KEYS):
        g = None
        for hd in range(PEER_HEADS):
            r = hd * N_KEYS + j * (E_CHUNK // N_KEYS) + il
            cnt = jnp.broadcast_to(cnt_ref[pl.ds(r, 1), :], (N_KEYS, tm)).astype(BF16)
            cc = jnp.broadcast_to(cc_ref[pl.ds(r, 1), :], (N_KEYS, tm)).astype(BF16)
            kr = slice(hd * N_KEYS, (hd + 1) * N_KEYS)
            e2 = e2_ref[kr, :]
            t = jnp.where(r2_ref[kr, :] < cnt, e2, jnp.zeros_like(e2)) * cc
            g = t if g is None else g + t
        parts.append(g * act[il * N_KEYS:(il + 1) * N_KEYS, :])
    p = jnp.concatenate(parts, axis=0)
    acc_ref[...] += _dot(vt_ref[...], p)

    @pl.when(j == nj - 1)
    def _():
        x2 = x1_ref[...] + mod_ref[0, 5:6, :] * acc_ref[...].T
        y_ref[...] = _rms(x2, g_ref[...])


def _peer_call(h2, x1, mod3, norm_final, wqt, k1, k2, u_b, vt_b, tok_off, n, seq):
    tm = TM_PEER
    nblk = n // tm
    off = tok_off // tm
    ne = u_b.shape[0] // E_CHUNK
    nq = PEER_HEADS * PEER_DQ
    nk = PEER_HEADS * N_KEYS

    def mod_idx(i, j):
        return (0 if seq is None else 1 + i // (seq // tm), 0, 0)

    const = lambda i, j: (0, 0)
    tok = lambda i, j: (off + i, 0)
    return pl.pallas_call(
        _peer_kernel,
        grid=(nblk, ne),
        in_specs=[
            pl.BlockSpec((tm, D_MODEL), tok),
            pl.BlockSpec((tm, D_MODEL), tok),
            pl.BlockSpec((1, 8, D_MODEL), mod_idx),
            pl.BlockSpec((1, D_MODEL), const),
            pl.BlockSpec(wqt.shape, const),
            pl.BlockSpec(k1.shape, const),
            pl.BlockSpec(k2.shape, const),
            pl.BlockSpec((E_CHUNK, D_MODEL), lambda i, j: (j, 0)),
            pl.BlockSpec((D_MODEL, E_CHUNK), lambda i, j: (0, j)),
        ],
        out_specs=pl.BlockSpec((tm, D_MODEL), lambda i, j: (i, 0)),
        out_shape=jax.ShapeDtypeStruct((n, D_MODEL), F32),
        scratch_shapes=[
            pltpu.VMEM((nq, tm), F32),
            pltpu.VMEM((nk, tm), BF16),
            pltpu.VMEM((nk, tm), BF16),
            pltpu.VMEM((nk, tm), F32),
            pltpu.VMEM((nk, tm), F32),
            pltpu.VMEM((D_MODEL, tm), F32),
        ],
        compiler_params=pltpu.CompilerParams(
            dimension_semantics=("parallel", "arbitrary"), vmem_limit_bytes=VMEM_LIMIT),
        name="peer_n%d" % n,
    )(h2, x1, mod3, norm_final, wqt, k1, k2, u_b, vt_b)


def _rope_tables(seq):
    rows = seq // GRID_W
    row = jnp.repeat(jnp.arange(rows, dtype=F32), GRID_W)
    col = jnp.tile(jnp.arange(GRID_W, dtype=F32), rows)
    nf = MLA_ROPE // 4
    freqs = jnp.power(ROPE_BASE, -jnp.arange(nf, dtype=F32) / nf)
    ang = jnp.concatenate([row[:, None] * freqs, col[:, None] * freqs], axis=-1)
    cos, sin = jnp.cos(ang), jnp.sin(ang)
    pad = jnp.zeros((seq, LANE - MLA_ROPE), F32)
    ctab = jnp.concatenate([cos, cos, pad], axis=1)
    stab = jnp.concatenate([-sin, sin, pad], axis=1)
    cid = jnp.concatenate([jnp.ones((TM, MLA_ROPE), F32), jnp.zeros((TM, LANE - MLA_ROPE), F32)], axis=1)
    ctab = jnp.concatenate([ctab, cid], axis=0)
    stab = jnp.concatenate([stab, jnp.zeros((TM, LANE), F32)], axis=0)
    return ctab, stab


def _pad_cols(a, width):
    return jnp.concatenate([a, jnp.zeros((a.shape[0], width - a.shape[1]), a.dtype)], axis=1)


def kernel(x_prompt, x_sample, c, cache_ckv, cache_krope, state_ret_fwd, state_ret_bwd, c_ctx, w_mod, b_mod, norm_mix, norm_ffn, norm_final, w_in, q_norm, kv_norm, w_uq, w_ukv, ret_logit_fwd, ret_logit_bwd, ret_gn, w_up_a, w_up_b, w_o, peer_wq, peer_keys1, peer_keys2, peer_u, peer_v):
    depth = w_mod.shape[0]
    assert depth == 1
    bp, sp, _ = x_prompt.shape
    bs, ss, _ = x_sample.shape
    past = cache_ckv.shape[2]
    n_p, n_s = bp * sp, bs * ss
    assert 1 + bs <= MOD_ROWS and sp == TM and ss % TM == 0 and ss % TM_PEER == 0 and n_p % TM_PEER == 0
    np_blocks, seq_blocks = n_p // TM, ss // TM
    l = 0

    half = MLA_ROPE // 2
    kr1 = w_in[l][:, 384:384 + half]
    kr2 = w_in[l][:, 384 + half:384 + MLA_ROPE]
    win_p = jnp.concatenate([
        w_in[l][:, :384],
        _pad_cols(jnp.concatenate([kr1, kr2], axis=1), LANE),
        _pad_cols(jnp.concatenate([kr2, kr1], axis=1), LANE),
        w_in[l][:, 384 + MLA_ROPE:],
    ], axis=1).astype(BF16)
    assert win_p.shape[1] == _D_IN_P
    w3 = w_uq[l].reshape(Q_LORA, MLA_HEADS, MLA_NOPE + MLA_ROPE)
    r1 = w3[:, :, MLA_NOPE:MLA_NOPE + half]
    r2 = w3[:, :, MLA_NOPE + half:]
    zpad = jnp.zeros((Q_LORA, MLA_HEADS, LANE - MLA_ROPE), F32)
    wuq_all = jnp.concatenate([
        w3[:, :, :MLA_NOPE].reshape(Q_LORA, MLA_HEADS * MLA_NOPE),
        jnp.concatenate([r1, r2, zpad], axis=2).reshape(Q_LORA, MLA_HEADS * LANE),
        jnp.concatenate([r2, r1, zpad], axis=2).reshape(Q_LORA, MLA_HEADS * LANE),
    ], axis=1).astype(BF16)
    wkv3 = w_ukv[l].reshape(KV_LORA, MLA_HEADS, MLA_NOPE + MLA_V)
    eye = jnp.eye(MLA_HEADS, dtype=F32)
    wk_hdl = jnp.transpose(wkv3[:, :, :MLA_NOPE], (1, 2, 0))
    wv_hld = jnp.transpose(wkv3[:, :, MLA_NOPE:], (1, 0, 2))
    bdk = (wk_hdl[:, :, None, :] * eye[:, None, :, None]).reshape(
        MLA_HEADS * MLA_NOPE, MLA_HEADS * KV_LORA).astype(BF16)
    bdv = (wv_hld[:, :, None, :] * eye[:, None, :, None]).reshape(
        MLA_HEADS * KV_LORA, MLA_HEADS * MLA_V).astype(BF16)
    wa = w_up_a[l].astype(BF16)
    wb = w_up_b[l].astype(BF16)
    wo = w_o[l].astype(BF16)
    wqt = peer_wq[l].T.astype(BF16)
    k1 = peer_keys1[l].reshape(PEER_HEADS * N_KEYS, PEER_DQ // 2).astype(BF16)
    k2 = peer_keys2[l].reshape(PEER_HEADS * N_KEYS, PEER_DQ // 2).astype(BF16)
    u_b = peer_u[l].astype(BF16)
    vt_b = peer_v[l].T.astype(BF16)
    ctab, stab = _rope_tables(ss)
    lf = jnp.broadcast_to(ret_logit_fwd[l][:, None, None], (RET_HEADS, 1, LANE))
    lb = jnp.broadcast_to(ret_logit_bwd[l][:, None, None], (RET_HEADS, 1, LANE))

    cc = jnp.concatenate([c_ctx[None, :], c, jnp.zeros((MOD_ROWS - 1 - bs, D_MODEL), F32)], axis=0)
    mod = _mod_call(cc, w_mod[l], b_mod[l][None, :])
    mod3 = jnp.concatenate([mod.reshape(MOD_ROWS, 6, D_MODEL), jnp.zeros((MOD_ROWS, 2, D_MODEL), F32)], axis=1)

    xp = x_prompt.reshape(n_p, D_MODEL)
    xs = x_sample.reshape(n_s, D_MODEL)
    qt, kk, ckvt, ckv32, kr32, rq, rk, rv, srg, sga, sgb = _proj_call(
        xp, xs, mod3, norm_mix[l][None, :], win_p, q_norm[l][None, :], kv_norm[l][None, :], wuq_all.T, bdk.T,
        ctab, stab, np_blocks, seq_blocks)

    kk_p = kk[:n_p].reshape(bp, sp, 2 * LANE)
    vt_p = jnp.transpose(ckvt[:, :n_p].reshape(KV_LORA, bp, sp), (1, 0, 2))
    cache_kk = jnp.concatenate([cache_ckv[:, l], _pad_cols(
        cache_krope[:, l].reshape(bs * past, MLA_ROPE), LANE).reshape(bs, past, LANE)], axis=2).astype(BF16)
    kk_s = jnp.concatenate([kk[n_p:].reshape(bs, ss, 2 * LANE), cache_kk], axis=1)
    vt_s = jnp.concatenate([jnp.transpose(ckvt[:, n_p:].reshape(KV_LORA, bs, ss), (1, 0, 2)),
                            jnp.transpose(cache_ckv[:, l], (0, 2, 1)).astype(BF16)], axis=2)
    attn_p = _attn_call(qt, kk_p, vt_p, bdv, 0, bp, sp, sp)
    attn_s = _attn_call(qt, kk_s, vt_s, bdv, n_p, bs, ss, 512)

    gn = ret_gn[l][None, :]
    ret_p, sf, sb = _ret_call(rq, rk, rv, srg, gn, lf, lb, None, None, 0, bp, sp)
    ret_s, _, _ = _ret_call(rq, rk, rv, srg, gn, lf, lb, state_ret_fwd[:, l], state_ret_bwd[:, l], n_p, bs, ss)

    x1, h2 = _merge_call(attn_p, attn_s, ret_p, ret_s, sga, sgb, xp, xs, mod3, norm_ffn[l][None, :],
                          wa, wb, wo, np_blocks, seq_blocks)

    nf = norm_final[None, :]
    y_prompt = _peer_call(h2, x1, mod3, nf, wqt, k1, k2, u_b, vt_b, 0, n_p, None).reshape(bp, sp, D_MODEL)
    y_sample = _peer_call(h2, x1, mod3, nf, wqt, k1, k2, u_b, vt_b, n_p, n_s, ss).reshape(bs, ss, D_MODEL)
    new_ckv = ckv32[:n_p].reshape(bp, 1, sp, KV_LORA)
    new_kr = kr32[:n_p, :MLA_ROPE].reshape(bp, 1, sp, MLA_ROPE)
    return (y_prompt, y_sample, new_ckv, new_kr, sf[:, None], sb[:, None])
```

```python
import functools

import jax
import jax.numpy as jnp
from jax import lax
from jax.experimental import pallas as pl
from jax.experimental.pallas import tpu as pltpu

F32 = jnp.float32
BF16 = jnp.bfloat16

D_MODEL = 1024
GRID_W = 64
EPS = 1e-6
MLA_HEADS = 8
MLA_NOPE = 64
MLA_ROPE = 32
MLA_V = 64
Q_LORA = 256
KV_LORA = 128
ROPE_BASE = 10000.0
RET_HEADS = 4
RET_DK = 128
RET_DV = 128
PEER_HEADS = 8
PEER_DQ = 256
N_KEYS = 128
PEER_TOPK = 16

LANE = 128
SUBLANES = 8
VMEM_LIMIT = 56 * 1024 * 1024

TM = 256
TQ = 128
RET_CHUNK = 256
TM_PEER = 512
E_CHUNK = 2048
MOD_ROWS = 16

_C_CQ = 0
_C_CKV = 256
_C_KR = 384
_C_KRS = 512
_C_RQ = 640
_C_RK = 1152
_C_RV = 1664
_C_RG = 2176
_C_GA = 2688
_C_GB = 3712
_D_IN_P = 4736


def _rms(x, g):
    return x * lax.rsqrt(jnp.mean(x * x, axis=-1, keepdims=True) + EPS) * g


def _dot(a, b):
    return jnp.dot(a, b, preferred_element_type=F32)


def _dot_nt(a, b):
    return lax.dot_general(a, b, (((1,), (1,)), ((), ())), preferred_element_type=F32)


def _mod_kernel(c_ref, w_ref, b_ref, o_ref):
    c = c_ref[...]
    s = c * jax.nn.sigmoid(c)
    o_ref[...] = _dot(s.astype(BF16), w_ref[...].astype(BF16)) + b_ref[...]


def _mod_call(cc, w_mod, b_mod):
    n = w_mod.shape[1]
    bn = 1024
    return pl.pallas_call(
        _mod_kernel,
        grid=(n // bn,),
        in_specs=[
            pl.BlockSpec((MOD_ROWS, D_MODEL), lambda j: (0, 0)),
            pl.BlockSpec((D_MODEL, bn), lambda j: (0, j)),
            pl.BlockSpec((1, bn), lambda j: (0, j)),
        ],
        out_specs=pl.BlockSpec((MOD_ROWS, bn), lambda j: (0, j)),
        out_shape=jax.ShapeDtypeStruct((MOD_ROWS, n), F32),
        compiler_params=pltpu.CompilerParams(dimension_semantics=("parallel",), vmem_limit_bytes=VMEM_LIMIT),
        name="mod",
    )(cc, w_mod, b_mod)


def _proj_kernel(np_blocks, xp_ref, xs_ref, mod_ref, g_ref, win_ref, qn_ref, kvn_ref, wuqt_ref, bdkt_ref,
                 ct_ref, st_ref, ctt_ref, stt_ref,
                 qt_ref, kk_ref, ckvt_ref, ckv_ref, kr_ref, rq_ref, rk_ref, rv_ref, srg_ref, sga_ref, sgb_ref):
    x = jnp.where(pl.program_id(0) < np_blocks, xp_ref[...], xs_ref[...])
    h = _rms(x, g_ref[...]) * (1.0 + mod_ref[0, 1:2, :]) + mod_ref[0, 0:1, :]
    y = _dot(h.astype(BF16), win_ref[...])

    ct = ct_ref[...]
    st = st_ref[...]
    ckv = _rms(y[:, _C_CKV:_C_CKV + KV_LORA], kvn_ref[...])
    kr = y[:, _C_KR:_C_KR + LANE]
    krot = kr * ct + y[:, _C_KRS:_C_KRS + LANE] * st
    ckv_ref[...] = ckv
    kr_ref[...] = kr
    kk_ref[:, 0:KV_LORA] = ckv.astype(BF16)
    kk_ref[:, KV_LORA:2 * KV_LORA] = krot.astype(BF16)
    ckvt_ref[...] = ckv.T.astype(BF16)

    cq = _rms(y[:, _C_CQ:_C_CQ + Q_LORA], qn_ref[...])
    qat = _dot(wuqt_ref[...], cq.T.astype(BF16))
    nn = MLA_HEADS * MLA_NOPE
    nr = MLA_HEADS * LANE
    qlat = _dot(bdkt_ref[...], qat[:nn, :].astype(BF16))
    ctt = ctt_ref[...]
    stt = stt_ref[...]
    scale = (MLA_NOPE + MLA_ROPE) ** -0.5
    for hd in range(MLA_HEADS):
        lo = hd * LANE
        qrot = qat[nn + lo:nn + lo + LANE, :] * ctt + qat[nn + nr + lo:nn + nr + lo + LANE, :] * stt
        qt_ref[2 * lo:2 * lo + LANE, :] = (qlat[lo:lo + LANE, :] * scale).astype(BF16)
        qt_ref[2 * lo + LANE:2 * lo + 2 * LANE, :] = (qrot * scale).astype(BF16)

    hk = RET_HEADS * RET_DK
    rq_ref[...] = y[:, _C_RQ:_C_RQ + hk].astype(BF16)
    rk_ref[...] = (y[:, _C_RK:_C_RK + hk] * (RET_DK ** -0.5)).astype(BF16)
    rv_ref[...] = y[:, _C_RV:_C_RV + hk].astype(BF16)
    rg = y[:, _C_RG:_C_RG + hk]
    srg_ref[...] = rg * jax.nn.sigmoid(rg)
    sga_ref[...] = jax.nn.sigmoid(y[:, _C_GA:_C_GA + D_MODEL])
    sgb_ref[...] = jax.nn.sigmoid(y[:, _C_GB:_C_GB + D_MODEL])


def _proj_call(xp, xs, mod3, norm_mix, win_p, q_norm, kv_norm, wuq_t, bdk_t, ctab, stab, np_blocks, seq_blocks):
    n = xp.shape[0] + xs.shape[0]
    nblk = n // TM
    rope_id_block = ctab.shape[0] // TM - 1

    def mod_idx(i):
        return (jnp.where(i < np_blocks, 0, 1 + (i - np_blocks) // seq_blocks), 0, 0)

    def rope_blk(i):
        return jnp.where(i < np_blocks, rope_id_block, (i - np_blocks) % seq_blocks)

    rope_idx = lambda i: (rope_blk(i), 0)
    rope_idx_t = lambda i: (0, rope_blk(i))
    const = lambda i: (0, 0)
    row = lambda i: (i, 0)
    col = lambda i: (0, i)
    hk = RET_HEADS * RET_DK
    out_shape = [
        jax.ShapeDtypeStruct((2 * MLA_HEADS * LANE, n), BF16),
        jax.ShapeDtypeStruct((n, 2 * LANE), BF16),
        jax.ShapeDtypeStruct((KV_LORA, n), BF16),
        jax.ShapeDtypeStruct((n, KV_LORA), F32),
        jax.ShapeDtypeStruct((n, LANE), F32),
        jax.ShapeDtypeStruct((n, hk), BF16),
        jax.ShapeDtypeStruct((n, hk), BF16),
        jax.ShapeDtypeStruct((n, hk), BF16),
        jax.ShapeDtypeStruct((n, hk), F32),
        jax.ShapeDtypeStruct((n, D_MODEL), F32),
        jax.ShapeDtypeStruct((n, D_MODEL), F32),
    ]
    out_specs = [pl.BlockSpec((s.shape[0], TM), col) if s.shape[1] == n else pl.BlockSpec((TM, s.shape[1]), row)
                 for s in out_shape]
    return pl.pallas_call(
        functools.partial(_proj_kernel, np_blocks),
        grid=(nblk,),
        in_specs=[
            pl.BlockSpec((TM, D_MODEL), lambda i: (jnp.minimum(i, np_blocks - 1), 0)),
            pl.BlockSpec((TM, D_MODEL), lambda i: (jnp.maximum(i - np_blocks, 0), 0)),
            pl.BlockSpec((1, 8, D_MODEL), mod_idx),
            pl.BlockSpec((1, D_MODEL), const),
            pl.BlockSpec(win_p.shape, const),
            pl.BlockSpec((1, Q_LORA), const),
            pl.BlockSpec((1, KV_LORA), const),
            pl.BlockSpec(wuq_t.shape, const),
            pl.BlockSpec(bdk_t.shape, const),
            pl.BlockSpec((TM, LANE), rope_idx),
            pl.BlockSpec((TM, LANE), rope_idx),
            pl.BlockSpec((LANE, TM), rope_idx_t),
            pl.BlockSpec((LANE, TM), rope_idx_t),
        ],
        out_specs=out_specs,
        out_shape=out_shape,
        compiler_params=pltpu.CompilerParams(dimension_semantics=("parallel",), vmem_limit_bytes=VMEM_LIMIT),
        name="proj",
    )(xp, xs, mod3, norm_mix, win_p, q_norm, kv_norm, wuq_t, bdk_t, ctab, stab, ctab.T, stab.T)


def _attn_kernel(nkb, tkb, qt_ref, k_ref, vt_ref, bdv_ref, o_ref, s_ref):
    fw = 2 * LANE
    qt = jnp.concatenate([qt_ref[hd * fw:(hd + 1) * fw, :] for hd in range(MLA_HEADS)], axis=1)
    m = None
    for kb in range(nkb):
        s = _dot(k_ref[0, kb * tkb:(kb + 1) * tkb, :], qt)
        s_ref[kb * tkb:(kb + 1) * tkb, :] = s
        mk = jnp.max(s, axis=0, keepdims=True)
        m = mk if m is None else jnp.maximum(m, mk)
    l = None
    acc = None
    for kb in range(nkb):
        p = jnp.exp(s_ref[kb * tkb:(kb + 1) * tkb, :] - m)
        lk = jnp.sum(p, axis=0, keepdims=True)
        ak = _dot(vt_ref[0, :, kb * tkb:(kb + 1) * tkb], p.astype(BF16))
        l = lk if l is None else l + lk
        acc = ak if acc is None else acc + ak
    o = acc / l
    ocat = jnp.concatenate([o[:, hd * TQ:(hd + 1) * TQ].T for hd in range(MLA_HEADS)], axis=1)
    o_ref[...] = _dot(ocat.astype(BF16), bdv_ref[...]).astype(BF16)


def _attn_call(qt, kk3, vt3, bdv, tok_off, batch, seq, tkb):
    nq = seq // TQ
    tk = kk3.shape[1]
    off = tok_off // TQ
    return pl.pallas_call(
        functools.partial(_attn_kernel, tk // tkb, tkb),
        grid=(batch, nq),
        in_specs=[
            pl.BlockSpec((qt.shape[0], TQ), lambda b, qi: (0, off + b * nq + qi)),
            pl.BlockSpec((1, tk, 2 * LANE), lambda b, qi: (b, 0, 0)),
            pl.BlockSpec((1, KV_LORA, tk), lambda b, qi: (b, 0, 0)),
            pl.BlockSpec(bdv.shape, lambda b, qi: (0, 0)),
        ],
        out_specs=pl.BlockSpec((TQ, MLA_HEADS * MLA_V), lambda b, qi: (b * nq + qi, 0)),
        out_shape=jax.ShapeDtypeStruct((batch * seq, MLA_HEADS * MLA_V), BF16),
        scratch_shapes=[pltpu.VMEM((tk, MLA_HEADS * TQ), F32)],
        compiler_params=pltpu.CompilerParams(
            dimension_semantics=("parallel", "parallel"), vmem_limit_bytes=VMEM_LIMIT),
        name="attn_b%d" % batch,
    )(qt, kk3, vt3, bdv)


def _log_sigmoid(x):
    return jnp.minimum(x, 0.0) - jnp.log(1.0 + jnp.exp(-jnp.abs(x)))


def _ret_kernel(has_state, seq, *refs):
    if has_state:
        (q_ref, k_ref, v_ref, srg_ref, gn_ref, lf_ref, lb_ref, s0f_ref, s0b_ref,
         o_ref, sf_ref, sb_ref, yf_ref, yb_ref) = refs
    else:
        (q_ref, k_ref, v_ref, srg_ref, gn_ref, lf_ref, lb_ref,
         o_ref, sf_ref, sb_ref, yf_ref, yb_ref) = refs
    c = RET_CHUNK
    n = seq // c
    lgf = _log_sigmoid(lf_ref[0, :, 0:1])
    lgb = _log_sigmoid(lb_ref[0, :, 0:1])
    ii = lax.broadcasted_iota(jnp.int32, (c, c), 0)
    jj = lax.broadcasted_iota(jnp.int32, (c, c), 1)
    diff = (ii - jj).astype(F32)
    dmat_f = jnp.where(diff >= 0, jnp.exp(lgf * jnp.maximum(diff, 0.0)), 0.0)
    dmat_b = jnp.where(diff <= 0, jnp.exp(lgb * jnp.maximum(-diff, 0.0)), 0.0)
    idx = lax.broadcasted_iota(jnp.int32, (c, 1), 0).astype(F32)
    qdec_f = jnp.exp(lgf * (idx + 1.0))
    kdec_f = jnp.exp(lgf * (c - 1.0 - idx))
    cdec_f = jnp.exp(lgf * c)
    qdec_b = jnp.exp(lgb * (c - idx))
    kdec_b = jnp.exp(lgb * idx)
    cdec_b = jnp.exp(lgb * c)

    def chunk(start, state, dmat, qdec, kdec, cdec, y_ref):
        q = q_ref[pl.ds(start, c), :]
        k = k_ref[pl.ds(start, c), :]
        v = v_ref[pl.ds(start, c), :]
        sc = _dot_nt(q, k) * dmat
        y = _dot(sc.astype(BF16), v) + _dot(q, state.astype(BF16)) * qdec
        y_ref[pl.ds(start, c), :] = y
        kdt = (k.astype(F32) * kdec).T.astype(BF16)
        return state * cdec + _dot(kdt, v)

    def body(t, carry):
        sf, sb = carry
        sf = chunk(pl.multiple_of(t * c, c), sf, dmat_f, qdec_f, kdec_f, cdec_f, yf_ref)
        sb = chunk(pl.multiple_of((n - 1 - t) * c, c), sb, dmat_b, qdec_b, kdec_b, cdec_b, yb_ref)
        return sf, sb

    if has_state:
        init = (s0f_ref[0, 0], s0b_ref[0, 0])
    else:
        init = (jnp.zeros((RET_DK, RET_DV), F32), jnp.zeros((RET_DK, RET_DV), F32))
    sf, sb = lax.fori_loop(0, n, body, init)
    sf_ref[0, 0] = sf
    sb_ref[0, 0] = sb

    y = yf_ref[...] + yb_ref[...]
    mu = jnp.mean(y, axis=-1, keepdims=True)
    yc = y - mu
    var = jnp.mean(yc * yc, axis=-1, keepdims=True)
    yn = yc * lax.rsqrt(var + EPS)
    o_ref[...] = (srg_ref[...] * (yn * gn_ref[...])).astype(BF16)


def _ret_call(rq, rk, rv, srg, gn, lf, lb, s0f, s0b, tok_off, batch, seq):
    off = tok_off // seq
    has_state = s0f is not None
    tokblk = lambda b, h: (off + b, h)
    st = lambda b, h: (b, h, 0, 0)
    in_specs = [
        pl.BlockSpec((seq, RET_DK), tokblk),
        pl.BlockSpec((seq, RET_DK), tokblk),
        pl.BlockSpec((seq, RET_DV), tokblk),
        pl.BlockSpec((seq, RET_DV), tokblk),
        pl.BlockSpec((1, RET_DV), lambda b, h: (0, h)),
        pl.BlockSpec((1, 1, LANE), lambda b, h: (h, 0, 0)),
        pl.BlockSpec((1, 1, LANE), lambda b, h: (h, 0, 0)),
    ]
    args = [rq, rk, rv, srg, gn, lf, lb]
    if has_state:
        in_specs += [pl.BlockSpec((1, 1, RET_DK, RET_DV), st)] * 2
        args += [s0f, s0b]
    st_shape = jax.ShapeDtypeStruct((batch, RET_HEADS, RET_DK, RET_DV), F32)
    return pl.pallas_call(
        functools.partial(_ret_kernel, has_state, seq),
        grid=(batch, RET_HEADS),
        in_specs=in_specs,
        out_specs=[
            pl.BlockSpec((seq, RET_DV), lambda b, h: (b, h)),
            pl.BlockSpec((1, 1, RET_DK, RET_DV), st),
            pl.BlockSpec((1, 1, RET_DK, RET_DV), st),
        ],
        out_shape=[jax.ShapeDtypeStruct((batch * seq, RET_HEADS * RET_DV), BF16), st_shape, st_shape],
        scratch_shapes=[pltpu.VMEM((seq, RET_DV), F32), pltpu.VMEM((seq, RET_DV), F32)],
        compiler_params=pltpu.CompilerParams(
            dimension_semantics=("parallel", "parallel"), vmem_limit_bytes=VMEM_LIMIT),
        name="ret_b%d" % batch,
    )(*args)


def _merge_kernel(np_blocks, ap_ref, as_ref, rp_ref, rs_ref, sga_ref, sgb_ref, xp_ref, xs_ref, mod_ref, g_ref,
                  wa_ref, wb_ref, wo_ref, x1_ref, h2_ref):
    is_prompt = pl.program_id(0) < np_blocks
    attn = jnp.where(is_prompt, ap_ref[...], as_ref[...])
    ret = jnp.where(is_prompt, rp_ref[...], rs_ref[...])
    x = jnp.where(is_prompt, xp_ref[...], xs_ref[...])
    m = sga_ref[...] * _dot(attn, wa_ref[...]) + sgb_ref[...] * _dot(ret, wb_ref[...])
    o = _dot(m.astype(BF16), wo_ref[...])
    x1 = x + mod_ref[0, 2:3, :] * o
    x1_ref[...] = x1
    h2 = _rms(x1, g_ref[...]) * (1.0 + mod_ref[0, 4:5, :]) + mod_ref[0, 3:4, :]
    h2_ref[...] = h2.astype(BF16)


def _merge_call(attn_p, attn_s, ret_p, ret_s, sga, sgb, xp, xs, mod3, norm_ffn, wa, wb, wo, np_blocks, seq_blocks):
    n = xp.shape[0] + xs.shape[0]
    nblk = n // TM
    const = lambda i: (0, 0)
    row = lambda i: (i, 0)
    p_idx = lambda i: (jnp.minimum(i, np_blocks - 1), 0)
    s_idx = lambda i: (jnp.maximum(i - np_blocks, 0), 0)

    def mod_idx(i):
        return (jnp.where(i < np_blocks, 0, 1 + (i - np_blocks) // seq_blocks), 0, 0)

    w = attn_p.shape[1]
    return pl.pallas_call(
        functools.partial(_merge_kernel, np_blocks),
        grid=(nblk,),
        in_specs=[
            pl.BlockSpec((TM, w), p_idx),
            pl.BlockSpec((TM, w), s_idx),
            pl.BlockSpec((TM, w), p_idx),
            pl.BlockSpec((TM, w), s_idx),
            pl.BlockSpec((TM, D_MODEL), row),
            pl.BlockSpec((TM, D_MODEL), row),
            pl.BlockSpec((TM, D_MODEL), p_idx),
            pl.BlockSpec((TM, D_MODEL), s_idx),
            pl.BlockSpec((1, 8, D_MODEL), mod_idx),
            pl.BlockSpec((1, D_MODEL), const),
            pl.BlockSpec(wa.shape, const),
            pl.BlockSpec(wb.shape, const),
            pl.BlockSpec(wo.shape, const),
        ],
        out_specs=[pl.BlockSpec((TM, D_MODEL), row), pl.BlockSpec((TM, D_MODEL), row)],
        out_shape=[jax.ShapeDtypeStruct((n, D_MODEL), F32), jax.ShapeDtypeStruct((n, D_MODEL), BF16)],
        compiler_params=pltpu.CompilerParams(dimension_semantics=("parallel",), vmem_limit_bytes=VMEM_LIMIT),
        name="merge",
    )(attn_p, attn_s, ret_p, ret_s, sga, sgb, xp, xs, mod3, norm_ffn, wa, wb, wo)


def _top_rows(s, k):
    rows = []
    for _ in range(k):
        m = jnp.max(s, axis=0, keepdims=True)
        rows.append(m)
        s = jnp.where(s == m, -jnp.inf, s)
    return rows


def _sort_pairs(n):
    pairs = []
    p = 1
    while p < n:
        k = p
        while k >= 1:
            for j in range(k % p, n - k, 2 * k):
                for i in range(min(k, n - j - k)):
                    if (i + j) // (2 * p) == (i + j + k) // (2 * p):
                        pairs.append((i + j, i + j + k))
            k //= 2
        p *= 2
    return pairs


def _exchange(vs, i, j):
    vs[i], vs[j] = jnp.maximum(vs[i], vs[j]), jnp.minimum(vs[i], vs[j])


def _top_sorted(blocks):
    k = len(blocks)
    vs = list(blocks)
    for i, j in _sort_pairs(k):
        _exchange(vs, i, j)
    shift = SUBLANES // 2
    while shift >= 1:
        other = [pltpu.roll(v, shift, 0) for v in vs]
        vs = [jnp.maximum(vs[i], other[k - 1 - i]) for i in range(k)]
        d = k // 2
        while d >= 1:
            for i in range(k):
                if i & d == 0:
                    _exchange(vs, i, i + d)
            d //= 2
        shift //= 2
    return vs


def _stack_rows(rows):
    n, cols = len(rows), rows[0].shape[1]
    ridx = lax.broadcasted_iota(jnp.int32, (n, cols), 0)
    out = jnp.broadcast_to(rows[0], (n, cols))
    for a in range(1, n):
        out = jnp.where(ridx == a, rows[a], out)
    return out


def _route_tile(s1, s2):
    nb = N_KEYS // SUBLANES
    assert nb == PEER_TOPK
    b1 = [s1[SUBLANES * i:SUBLANES * (i + 1), :] for i in range(nb)]
    b2 = [s2[SUBLANES * i:SUBLANES * (i + 1), :] for i in range(nb)]
    t1 = _top_sorted(b1)
    t2 = _top_sorted(b2)
    r1 = [t[0:1, :] for t in t1]
    r2 = [t[0:1, :] for t in t2]
    v1 = _stack_rows(r1)
    v2 = _stack_rows(r2)
    k8 = PEER_TOPK // 2
    cands = [r1[0] + v2, r1[1] + v2[0:k8, :], r1[2] + v2[0:k8, :], r1[3] + v2[0:k8, :],
             v1 + r2[0], v1[0:k8, :] + r2[1], v1[0:k8, :] + r2[2]]
    tau = _top_rows(jnp.concatenate(cands, axis=0), PEER_TOPK)[-1]
    e1 = [jnp.exp(r - r1[0]) for r in r1]
    e2v = jnp.exp(v2 - r2[0])
    cnts, zs = [], []
    for a in range(PEER_TOPK):
        sel = (r1[a] + v2) >= tau
        cnts.append(jnp.sum(jnp.where(sel, 1.0, 0.0), axis=0, keepdims=True))
        zs.append(jnp.sum(jnp.where(sel, e2v, 0.0), axis=0, keepdims=True))
    z = e1[0] * zs[0]
    for a in range(1, PEER_TOPK):
        z = z + e1[a] * zs[a]
    zinv = 1.0 / z
    cnts = [jnp.broadcast_to(c, b1[0].shape) for c in cnts]
    rank2, cnt = [], []
    for i in range(nb):
        rk = jnp.zeros(b2[i].shape, F32)
        ct = jnp.zeros(b1[i].shape, F32)
        for a in range(PEER_TOPK):
            rk = jnp.where(t2[a] > b2[i], float(a + 1), rk)
            ct = jnp.where(b1[i] == t1[a], cnts[a], ct)
        rank2.append(rk)
        cnt.append(ct)
    rank2 = jnp.concatenate(rank2, axis=0)
    cnt = jnp.concatenate(cnt, axis=0)
    e2 = jnp.exp(s2 - r2[0])
    cc = jnp.exp(s1 - r1[0]) * zinv
    return rank2, e2, cnt, cc


def _peer_route(qt_ref, k1_ref, k2_ref, r2_ref, e2_ref, cnt_ref, cc_ref):
    half = PEER_DQ // 2
    tm = qt_ref.shape[1]

    def head(hd, carry):
        q0 = pl.multiple_of(hd * PEER_DQ, PEER_DQ)
        k0 = pl.multiple_of(hd * N_KEYS, N_KEYS)
        k1 = k1_ref[pl.ds(k0, N_KEYS), :]
        k2 = k2_ref[pl.ds(k0, N_KEYS), :]
        for st in range(tm // LANE):
            lanes = slice(st * LANE, (st + 1) * LANE)
            s1 = _dot(k1, qt_ref[pl.ds(q0, half), lanes].astype(BF16))
            s2 = _dot(k2, qt_ref[pl.ds(q0 + half, half), lanes].astype(BF16))
            rank2, e2, cnt, cc = _route_tile(s1, s2)
            r2_ref[pl.ds(k0, N_KEYS), lanes] = rank2.astype(BF16)
            e2_ref[pl.ds(k0, N_KEYS), lanes] = e2.astype(BF16)
            cnt_ref[pl.ds(k0, N_KEYS), lanes] = cnt
            cc_ref[pl.ds(k0, N_KEYS), lanes] = cc
        return carry

    lax.fori_loop(0, PEER_HEADS, head, 0)


def _peer_kernel(h2_ref, x1_ref, mod_ref, g_ref, wqt_ref, k1_ref, k2_ref, u_ref, vt_ref,
                 y_ref, qt_ref, r2_ref, e2_ref, cnt_ref, cc_ref, acc_ref):
    j = pl.program_id(1)
    nj = pl.num_programs(1)
    tm = h2_ref.shape[0]

    @pl.when(j == 0)
    def _():
        qt_ref[...] = _dot_nt(wqt_ref[...], h2_ref[...])
        _peer_route(qt_ref, k1_ref, k2_ref, r2_ref, e2_ref, cnt_ref, cc_ref)
        acc_ref[...] = jnp.zeros(acc_ref.shape, F32)

    hu = _dot_nt(u_ref[...], h2_ref[...])
    act = jax.nn.gelu(hu.astype(BF16), approximate=True)
    parts = []
    for il in range(E_CHUNK // N_KEYS):
        g = None
        for hd in range(PEER_HEADS):
            r = hd * N_KEYS + j * (E_CHUNK // N_KEYS) + il
            cnt = jnp.broadcast_to(cnt_ref[pl.ds(r, 1), :], (N_KEYS, tm)).astype(BF16)
            cc = jnp.broadcast_to(cc_ref[pl.ds(r, 1), :], (N_KEYS, tm)).astype(BF16)
            kr = slice(hd * N_KEYS, (hd + 1) * N_KEYS)
            e2 = e2_ref[kr, :]
            t = jnp.where(r2_ref[kr, :] < cnt, e2, jnp.zeros_like(e2)) * cc
            g = t if g is None else g + t
        parts.append(g * act[il * N_KEYS:(il + 1) * N_KEYS, :])
    p = jnp.concatenate(parts, axis=0)
    acc_ref[...] += _dot(vt_ref[...], p)

    @pl.when(j == nj - 1)
    def _():
        x2 = x1_ref[...] + mod_ref[0, 5:6, :] * acc_ref[...].T
        y_ref[...] = _rms(x2, g_ref[...])


def _peer_call(h2, x1, mod3, norm_final, wqt, k1, k2, u_b, vt_b, tok_off, n, seq):
    tm = TM_PEER
    nblk = n // tm
    off = tok_off // tm
    ne = u_b.shape[0] // E_CHUNK
    nq = PEER_HEADS * PEER_DQ
    nk = PEER_HEADS * N_KEYS

    def mod_idx(i, j):
        return (0 if seq is None else 1 + i // (seq // tm), 0, 0)

    const = lambda i, j: (0, 0)
    tok = lambda i, j: (off + i, 0)
    return pl.pallas_call(
        _peer_kernel,
        grid=(nblk, ne),
        in_specs=[
            pl.BlockSpec((tm, D_MODEL), tok),
            pl.BlockSpec((tm, D_MODEL), tok),
            pl.BlockSpec((1, 8, D_MODEL), mod_idx),
            pl.BlockSpec((1, D_MODEL), const),
            pl.BlockSpec(wqt.shape, const),
            pl.BlockSpec(k1.shape, const),
            pl.BlockSpec(k2.shape, const),
            pl.BlockSpec((E_CHUNK, D_MODEL), lambda i, j: (j, 0)),
            pl.BlockSpec((D_MODEL, E_CHUNK), lambda i, j: (0, j)),
        ],
        out_specs=pl.BlockSpec((tm, D_MODEL), lambda i, j: (i, 0)),
        out_shape=jax.ShapeDtypeStruct((n, D_MODEL), F32),
        scratch_shapes=[
            pltpu.VMEM((nq, tm), F32),
            pltpu.VMEM((nk, tm), BF16),
            pltpu.VMEM((nk, tm), BF16),
            pltpu.VMEM((nk, tm), F32),
            pltpu.VMEM((nk, tm), F32),
            pltpu.VMEM((D_MODEL, tm), F32),
        ],
        compiler_params=pltpu.CompilerParams(
            dimension_semantics=("parallel", "arbitrary"), vmem_limit_bytes=VMEM_LIMIT),
        name="peer_n%d" % n,
    )(h2, x1, mod3, norm_final, wqt, k1, k2, u_b, vt_b)


def _rope_tables(seq):
    rows = seq // GRID_W
    row = jnp.repeat(jnp.arange(rows, dtype=F32), GRID_W)
    col = jnp.tile(jnp.arange(GRID_W, dtype=F32), rows)
    nf = MLA_ROPE // 4
    freqs = jnp.power(ROPE_BASE, -jnp.arange(nf, dtype=F32) / nf)
    ang = jnp.concatenate([row[:, None] * freqs, col[:, None] * freqs], axis=-1)
    cos, sin = jnp.cos(ang), jnp.sin(ang)
    pad = jnp.zeros((seq, LANE - MLA_ROPE), F32)
    ctab = jnp.concatenate([cos, cos, pad], axis=1)
    stab = jnp.concatenate([-sin, sin, pad], axis=1)
    cid = jnp.concatenate([jnp.ones((TM, MLA_ROPE), F32), jnp.zeros((TM, LANE - MLA_ROPE), F32)], axis=1)
    ctab = jnp.concatenate([ctab, cid], axis=0)
    stab = jnp.concatenate([stab, jnp.zeros((TM, LANE), F32)], axis=0)
    return ctab, stab


def _pad_cols(a, width):
    return jnp.concatenate([a, jnp.zeros((a.shape[0], width - a.shape[1]), a.dtype)], axis=1)


def kernel(x_prompt, x_sample, c, cache_ckv, cache_krope, state_ret_fwd, state_ret_bwd, c_ctx, w_mod, b_mod, norm_mix, norm_ffn, norm_final, w_in, q_norm, kv_norm, w_uq, w_ukv, ret_logit_fwd, ret_logit_bwd, ret_gn, w_up_a, w_up_b, w_o, peer_wq, peer_keys1, peer_keys2, peer_u, peer_v):
    depth = w_mod.shape[0]
    assert depth == 1
    bp, sp, _ = x_prompt.shape
    bs, ss, _ = x_sample.shape
    past = cache_ckv.shape[2]
    n_p, n_s = bp * sp, bs * ss
    assert 1 + bs <= MOD_ROWS and sp == TM and ss % TM == 0 and ss % TM_PEER == 0 and n_p % TM_PEER == 0
    np_blocks, seq_blocks = n_p // TM, ss // TM
    l = 0

    half = MLA_ROPE // 2
    kr1 = w_in[l][:, 384:384 + half]
    kr2 = w_in[l][:, 384 + half:384 + MLA_ROPE]
    win_p = jnp.concatenate([
        w_in[l][:, :384],
        _pad_cols(jnp.concatenate([kr1, kr2], axis=1), LANE),
        _pad_cols(jnp.concatenate([kr2, kr1], axis=1), LANE),
        w_in[l][:, 384 + MLA_ROPE:],
    ], axis=1).astype(BF16)
    assert win_p.shape[1] == _D_IN_P
    w3 = w_uq[l].reshape(Q_LORA, MLA_HEADS, MLA_NOPE + MLA_ROPE)
    r1 = w3[:, :, MLA_NOPE:MLA_NOPE + half]
    r2 = w3[:, :, MLA_NOPE + half:]
    zpad = jnp.zeros((Q_LORA, MLA_HEADS, LANE - MLA_ROPE), F32)
    wuq_all = jnp.concatenate([
        w3[:, :, :MLA_NOPE].reshape(Q_LORA, MLA_HEADS * MLA_NOPE),
        jnp.concatenate([r1, r2, zpad], axis=2).reshape(Q_LORA, MLA_HEADS * LANE),
        jnp.concatenate([r2, r1, zpad], axis=2).reshape(Q_LORA, MLA_HEADS * LANE),
    ], axis=1).astype(BF16)
    wkv3 = w_ukv[l].reshape(KV_LORA, MLA_HEADS, MLA_NOPE + MLA_V)
    eye = jnp.eye(MLA_HEADS, dtype=F32)
    wk_hdl = jnp.transpose(wkv3[:, :, :MLA_NOPE], (1, 2, 0))
    wv_hld = jnp.transpose(wkv3[:, :, MLA_NOPE:], (1, 0, 2))
    bdk = (wk_hdl[:, :, None, :] * eye[:, None, :, None]).reshape(
        MLA_HEADS * MLA_NOPE, MLA_HEADS * KV_LORA).astype(BF16)
    bdv = (wv_hld[:, :, None, :] * eye[:, None, :, None]).reshape(
        MLA_HEADS * KV_LORA, MLA_HEADS * MLA_V).astype(BF16)
    wa = w_up_a[l].astype(BF16)
    wb = w_up_b[l].astype(BF16)
    wo = w_o[l].astype(BF16)
    wqt = peer_wq[l].T.astype(BF16)
    k1 = peer_keys1[l].reshape(PEER_HEADS * N_KEYS, PEER_DQ // 2).astype(BF16)
    k2 = peer_keys2[l].reshape(PEER_HEADS * N_KEYS, PEER_DQ // 2).astype(BF16)
    u_b = peer_u[l].astype(BF16)
    vt_b = peer_v[l].T.astype(BF16)
    ctab, stab = _rope_tables(ss)
    lf = jnp.broadcast_to(ret_logit_fwd[l][:, None, None], (RET_HEADS, 1, LANE))
    lb = jnp.broadcast_to(ret_logit_bwd[l][:, None, None], (RET_HEADS, 1, LANE))

    cc = jnp.concatenate([c_ctx[None, :], c, jnp.zeros((MOD_ROWS - 1 - bs, D_MODEL), F32)], axis=0)
    mod = _mod_call(cc, w_mod[l], b_mod[l][None, :])
    mod3 = jnp.concatenate([mod.reshape(MOD_ROWS, 6, D_MODEL), jnp.zeros((MOD_ROWS, 2, D_MODEL), F32)], axis=1)

    xp = x_prompt.reshape(n_p, D_MODEL)
    xs = x_sample.reshape(n_s, D_MODEL)
    qt, kk, ckvt, ckv32, kr32, rq, rk, rv, srg, sga, sgb = _proj_call(
        xp, xs, mod3, norm_mix[l][None, :], win_p, q_norm[l][None, :], kv_norm[l][None, :], wuq_all.T, bdk.T,
        ctab, stab, np_blocks, seq_blocks)

    kk_p = kk[:n_p].reshape(bp, sp, 2 * LANE)
    vt_p = jnp.transpose(ckvt[:, :n_p].reshape(KV_LORA, bp, sp), (1, 0, 2))
    cache_kk = jnp.concatenate([cache_ckv[:, l], _pad_cols(
        cache_krope[:, l].reshape(bs * past, MLA_ROPE), LANE).reshape(bs, past, LANE)], axis=2).astype(BF16)
    kk_s = jnp.concatenate([kk[n_p:].reshape(bs, ss, 2 * LANE), cache_kk], axis=1)
    vt_s = jnp.concatenate([jnp.transpose(ckvt[:, n_p:].reshape(KV_LORA, bs, ss), (1, 0, 2)),
                            jnp.transpose(cache_ckv[:, l], (0, 2, 1)).astype(BF16)], axis=2)
    attn_p = _attn_call(qt, kk_p, vt_p, bdv, 0, bp, sp, sp)
    attn_s = _attn_call(qt, kk_s, vt_s, bdv, n_p, bs, ss, 512)

    gn = ret_gn[l][None, :]
    ret_p, sf, sb = _ret_call(rq, rk, rv, srg, gn, lf, lb, None, None, 0, bp, sp)
    ret_s, _, _ = _ret_call(rq, rk, rv, srg, gn, lf, lb, state_ret_fwd[:, l], state_ret_bwd[:, l], n_p, bs, ss)

    x1, h2 = _merge_call(attn_p, attn_s, ret_p, ret_s, sga, sgb, xp, xs, mod3, norm_ffn[l][None, :],
                          wa, wb, wo, np_blocks, seq_blocks)

    nf = norm_final[None, :]
    y_prompt = _peer_call(h2, x1, mod3, nf, wqt, k1, k2, u_b, vt_b, 0, n_p, None).reshape(bp, sp, D_MODEL)
    y_sample = _peer_call(h2, x1, mod3, nf, wqt, k1, k2, u_b, vt_b, n_p, n_s, ss).reshape(bs, ss, D_MODEL)
    new_ckv = ckv32[:n_p].reshape(bp, 1, sp, KV_LORA)
    new_kr = kr32[:n_p, :MLA_ROPE].reshape(bp, 1, sp, MLA_ROPE)
    return (y_prompt, y_sample, new_ckv, new_kr, sf[:, None], sb[:, None])
```

```python
import functools

import jax
import jax.numpy as jnp
from jax import lax
from jax.experimental import pallas as pl
from jax.experimental.pallas import tpu as pltpu

F32 = jnp.float32
BF16 = jnp.bfloat16

D_MODEL = 1024
GRID_W = 64
EPS = 1e-6
MLA_HEADS = 8
MLA_NOPE = 64
MLA_ROPE = 32
MLA_V = 64
Q_LORA = 256
KV_LORA = 128
ROPE_BASE = 10000.0
RET_HEADS = 4
RET_DK = 128
RET_DV = 128
PEER_HEADS = 8
PEER_DQ = 256
N_KEYS = 128
PEER_TOPK = 16

LANE = 128
SUBLANES = 8
LOG2_E = 1.4426950408889634
VMEM_LIMIT = 56 * 1024 * 1024

TM = 256
TQ = 128
RET_CHUNK = 256
TM_PEER = 512
E_CHUNK = 2048
PEER_KEYS_PER_DOT = 2
MOD_ROWS = 16

_C_CQ = 0
_C_CKV = 256
_C_KR = 384
_C_KRS = 512
_C_RQ = 640
_C_RK = 1152
_C_RV = 1664
_C_RG = 2176
_C_GA = 2688
_C_GB = 3712
_D_IN_P = 4736


def _rms(x, g):
    return x * lax.rsqrt(jnp.mean(x * x, axis=-1, keepdims=True) + EPS) * g


def _dot(a, b):
    return jnp.dot(a, b, preferred_element_type=F32)


def _dot_nt(a, b):
    return lax.dot_general(a, b, (((1,), (1,)), ((), ())), preferred_element_type=F32)


def _mod_kernel(c_ref, w_ref, b_ref, o_ref):
    c = c_ref[...]
    s = c * jax.nn.sigmoid(c)
    o_ref[...] = _dot(s.astype(BF16), w_ref[...].astype(BF16)) + b_ref[...]


def _mod_call(cc, w_mod, b_mod):
    n = w_mod.shape[1]
    bn = 1024
    return pl.pallas_call(
        _mod_kernel,
        grid=(n // bn,),
        in_specs=[
            pl.BlockSpec((MOD_ROWS, D_MODEL), lambda j: (0, 0)),
            pl.BlockSpec((D_MODEL, bn), lambda j: (0, j)),
            pl.BlockSpec((1, bn), lambda j: (0, j)),
        ],
        out_specs=pl.BlockSpec((MOD_ROWS, bn), lambda j: (0, j)),
        out_shape=jax.ShapeDtypeStruct((MOD_ROWS, n), F32),
        compiler_params=pltpu.CompilerParams(dimension_semantics=("parallel",), vmem_limit_bytes=VMEM_LIMIT),
        name="mod",
    )(cc, w_mod, b_mod)


def _proj_kernel(np_blocks, xp_ref, xs_ref, mod_ref, g_ref, win_ref, qn_ref, kvn_ref, wuqt_ref, bdkt_ref,
                 ct_ref, st_ref, ctt_ref, stt_ref,
                 qt_ref, kk_ref, ckvt_ref, ckv_ref, kr_ref, rq_ref, rk_ref, rv_ref, srg_ref, sga_ref, sgb_ref):
    x = jnp.where(pl.program_id(0) < np_blocks, xp_ref[...], xs_ref[...])
    h = _rms(x, g_ref[...]) * (1.0 + mod_ref[0, 1:2, :]) + mod_ref[0, 0:1, :]
    y = _dot(h.astype(BF16), win_ref[...])

    ct = ct_ref[...]
    st = st_ref[...]
    ckv = _rms(y[:, _C_CKV:_C_CKV + KV_LORA], kvn_ref[...])
    kr = y[:, _C_KR:_C_KR + LANE]
    krot = kr * ct + y[:, _C_KRS:_C_KRS + LANE] * st
    ckv_ref[...] = ckv
    kr_ref[...] = kr
    kk_ref[:, 0:KV_LORA] = ckv.astype(BF16)
    kk_ref[:, KV_LORA:2 * KV_LORA] = krot.astype(BF16)
    ckvt_ref[...] = ckv.T.astype(BF16)

    cq = _rms(y[:, _C_CQ:_C_CQ + Q_LORA], qn_ref[...])
    qat = _dot(wuqt_ref[...], cq.T.astype(BF16))
    nn = MLA_HEADS * MLA_NOPE
    nr = MLA_HEADS * LANE
    qlat = _dot(bdkt_ref[...], qat[:nn, :].astype(BF16))
    ctt = ctt_ref[...]
    stt = stt_ref[...]
    scale = (MLA_NOPE + MLA_ROPE) ** -0.5 * LOG2_E
    for hd in range(MLA_HEADS):
        lo = hd * LANE
        qrot = qat[nn + lo:nn + lo + LANE, :] * ctt + qat[nn + nr + lo:nn + nr + lo + LANE, :] * stt
        qt_ref[2 * lo:2 * lo + LANE, :] = (qlat[lo:lo + LANE, :] * scale).astype(BF16)
        qt_ref[2 * lo + LANE:2 * lo + 2 * LANE, :] = (qrot * scale).astype(BF16)

    hk = RET_HEADS * RET_DK
    rq_ref[...] = y[:, _C_RQ:_C_RQ + hk].astype(BF16)
    rk_ref[...] = (y[:, _C_RK:_C_RK + hk] * (RET_DK ** -0.5)).astype(BF16)
    rv_ref[...] = y[:, _C_RV:_C_RV + hk].astype(BF16)
    rg = y[:, _C_RG:_C_RG + hk]
    srg_ref[...] = (rg * jax.nn.sigmoid(rg)).astype(BF16)
    sga_ref[...] = jax.nn.sigmoid(y[:, _C_GA:_C_GA + D_MODEL]).astype(BF16)
    sgb_ref[...] = jax.nn.sigmoid(y[:, _C_GB:_C_GB + D_MODEL]).astype(BF16)


def _proj_call(xp, xs, mod3, norm_mix, win_p, q_norm, kv_norm, wuq_t, bdk_t, ctab, stab, np_blocks, seq_blocks):
    n = xp.shape[0] + xs.shape[0]
    nblk = n // TM
    rope_id_block = ctab.shape[0] // TM - 1

    def mod_idx(i):
        return (jnp.where(i < np_blocks, 0, 1 + (i - np_blocks) // seq_blocks), 0, 0)

    def rope_blk(i):
        return jnp.where(i < np_blocks, rope_id_block, (i - np_blocks) % seq_blocks)

    rope_idx = lambda i: (rope_blk(i), 0)
    rope_idx_t = lambda i: (0, rope_blk(i))
    const = lambda i: (0, 0)
    row = lambda i: (i, 0)
    col = lambda i: (0, i)
    hk = RET_HEADS * RET_DK
    out_shape = [
        jax.ShapeDtypeStruct((2 * MLA_HEADS * LANE, n), BF16),
        jax.ShapeDtypeStruct((n, 2 * LANE), BF16),
        jax.ShapeDtypeStruct((KV_LORA, n), BF16),
        jax.ShapeDtypeStruct((n, KV_LORA), F32),
        jax.ShapeDtypeStruct((n, LANE), F32),
        jax.ShapeDtypeStruct((n, hk), BF16),
        jax.ShapeDtypeStruct((n, hk), BF16),
        jax.ShapeDtypeStruct((n, hk), BF16),
        jax.ShapeDtypeStruct((n, hk), BF16),
        jax.ShapeDtypeStruct((n, D_MODEL), BF16),
        jax.ShapeDtypeStruct((n, D_MODEL), BF16),
    ]
    out_specs = [pl.BlockSpec((s.shape[0], TM), col) if s.shape[1] == n else pl.BlockSpec((TM, s.shape[1]), row)
                 for s in out_shape]
    return pl.pallas_call(
        functools.partial(_proj_kernel, np_blocks),
        grid=(nblk,),
        in_specs=[
            pl.BlockSpec((TM, D_MODEL), lambda i: (jnp.minimum(i, np_blocks - 1), 0)),
            pl.BlockSpec((TM, D_MODEL), lambda i: (jnp.maximum(i - np_blocks, 0), 0)),
            pl.BlockSpec((1, 8, D_MODEL), mod_idx),
            pl.BlockSpec((1, D_MODEL), const),
            pl.BlockSpec(win_p.shape, const),
            pl.BlockSpec((1, Q_LORA), const),
            pl.BlockSpec((1, KV_LORA), const),
            pl.BlockSpec(wuq_t.shape, const),
            pl.BlockSpec(bdk_t.shape, const),
            pl.BlockSpec((TM, LANE), rope_idx),
            pl.BlockSpec((TM, LANE), rope_idx),
            pl.BlockSpec((LANE, TM), rope_idx_t),
            pl.BlockSpec((LANE, TM), rope_idx_t),
        ],
        out_specs=out_specs,
        out_shape=out_shape,
        compiler_params=pltpu.CompilerParams(dimension_semantics=("parallel",), vmem_limit_bytes=VMEM_LIMIT),
        name="proj",
    )(xp, xs, mod3, norm_mix, win_p, q_norm, kv_norm, wuq_t, bdk_t, ctab, stab, ctab.T, stab.T)


def _attn_kernel(nkb, tkb, qt_ref, k_ref, vt_ref, bdv_ref, o_ref, s_ref):
    fw = 2 * LANE
    qt = jnp.concatenate([qt_ref[hd * fw:(hd + 1) * fw, :] for hd in range(MLA_HEADS)], axis=1)
    m = None
    for kb in range(nkb):
        s = _dot(k_ref[0, kb * tkb:(kb + 1) * tkb, :], qt)
        s_ref[kb * tkb:(kb + 1) * tkb, :] = s
        mk = jnp.max(s, axis=0, keepdims=True)
        m = mk if m is None else jnp.maximum(m, mk)
    l = None
    acc = None
    for kb in range(nkb):
        p = jnp.exp2(s_ref[kb * tkb:(kb + 1) * tkb, :] - m)
        lk = jnp.sum(p, axis=0, keepdims=True)
        ak = _dot(vt_ref[0, :, kb * tkb:(kb + 1) * tkb], p.astype(BF16))
        l = lk if l is None else l + lk
        acc = ak if acc is None else acc + ak
    o = acc / l
    ocat = jnp.concatenate([o[:, hd * TQ:(hd + 1) * TQ].T for hd in range(MLA_HEADS)], axis=1)
    o_ref[...] = _dot(ocat.astype(BF16), bdv_ref[...]).astype(BF16)


def _attn_call(qt, kk3, vt3, bdv, tok_off, batch, seq, tkb):
    nq = seq // TQ
    tk = kk3.shape[1]
    off = tok_off // TQ
    return pl.pallas_call(
        functools.partial(_attn_kernel, tk // tkb, tkb),
        grid=(batch, nq),
        in_specs=[
            pl.BlockSpec((qt.shape[0], TQ), lambda b, qi: (0, off + b * nq + qi)),
            pl.BlockSpec((1, tk, 2 * LANE), lambda b, qi: (b, 0, 0)),
            pl.BlockSpec((1, KV_LORA, tk), lambda b, qi: (b, 0, 0)),
            pl.BlockSpec(bdv.shape, lambda b, qi: (0, 0)),
        ],
        out_specs=pl.BlockSpec((TQ, MLA_HEADS * MLA_V), lambda b, qi: (b * nq + qi, 0)),
        out_shape=jax.ShapeDtypeStruct((batch * seq, MLA_HEADS * MLA_V), BF16),
        scratch_shapes=[pltpu.VMEM((tk, MLA_HEADS * TQ), F32)],
        compiler_params=pltpu.CompilerParams(
            dimension_semantics=("parallel", "parallel"), vmem_limit_bytes=VMEM_LIMIT),
        name="attn_b%d" % batch,
    )(qt, kk3, vt3, bdv)


def _log_sigmoid(x):
    return jnp.minimum(x, 0.0) - jnp.log(1.0 + jnp.exp(-jnp.abs(x)))


def _ret_kernel(has_state, seq, *refs):
    if has_state:
        (q_ref, k_ref, v_ref, srg_ref, gn_ref, lf_ref, lb_ref, s0f_ref, s0b_ref,
         o_ref, sf_ref, sb_ref, yf_ref, yb_ref) = refs
    else:
        (q_ref, k_ref, v_ref, srg_ref, gn_ref, lf_ref, lb_ref,
         o_ref, sf_ref, sb_ref, yf_ref, yb_ref) = refs
    c = RET_CHUNK
    n = seq // c
    lgf = _log_sigmoid(lf_ref[0, :, 0:1])
    lgb = _log_sigmoid(lb_ref[0, :, 0:1])
    ii = lax.broadcasted_iota(jnp.int32, (c, c), 0)
    jj = lax.broadcasted_iota(jnp.int32, (c, c), 1)
    diff = (ii - jj).astype(F32)
    dmat_f = jnp.where(diff >= 0, jnp.exp(lgf * jnp.maximum(diff, 0.0)), 0.0)
    dmat_b = jnp.where(diff <= 0, jnp.exp(lgb * jnp.maximum(-diff, 0.0)), 0.0)
    idx = lax.broadcasted_iota(jnp.int32, (c, 1), 0).astype(F32)
    qdec_f = jnp.exp(lgf * (idx + 1.0))
    kdec_f = jnp.exp(lgf * (c - 1.0 - idx))
    cdec_f = jnp.exp(lgf * c)
    qdec_b = jnp.exp(lgb * (c - idx))
    kdec_b = jnp.exp(lgb * idx)
    cdec_b = jnp.exp(lgb * c)

    def chunk(start, state, dmat, qdec, kdec, cdec, y_ref):
        q = q_ref[pl.ds(start, c), :]
        k = k_ref[pl.ds(start, c), :]
        v = v_ref[pl.ds(start, c), :]
        sc = _dot_nt(q, k) * dmat
        y = _dot(sc.astype(BF16), v) + _dot(q, state.astype(BF16)) * qdec
        y_ref[pl.ds(start, c), :] = y
        kdt = (k.astype(F32) * kdec).T.astype(BF16)
        return state * cdec + _dot(kdt, v)

    def body(t, carry):
        sf, sb = carry
        sf = chunk(pl.multiple_of(t * c, c), sf, dmat_f, qdec_f, kdec_f, cdec_f, yf_ref)
        sb = chunk(pl.multiple_of((n - 1 - t) * c, c), sb, dmat_b, qdec_b, kdec_b, cdec_b, yb_ref)
        return sf, sb

    if has_state:
        init = (s0f_ref[0, 0], s0b_ref[0, 0])
    else:
        init = (jnp.zeros((RET_DK, RET_DV), F32), jnp.zeros((RET_DK, RET_DV), F32))
    sf, sb = lax.fori_loop(0, n, body, init)
    sf_ref[0, 0] = sf
    sb_ref[0, 0] = sb

    y = yf_ref[...] + yb_ref[...]
    mu = jnp.mean(y, axis=-1, keepdims=True)
    yc = y - mu
    var = jnp.mean(yc * yc, axis=-1, keepdims=True)
    yn = yc * lax.rsqrt(var + EPS)
    o_ref[...] = (srg_ref[...] * (yn * gn_ref[...])).astype(BF16)


def _ret_call(rq, rk, rv, srg, gn, lf, lb, s0f, s0b, tok_off, batch, seq):
    off = tok_off // seq
    has_state = s0f is not None
    tokblk = lambda b, h: (off + b, h)
    st = lambda b, h: (b, h, 0, 0)
    in_specs = [
        pl.BlockSpec((seq, RET_DK), tokblk),
        pl.BlockSpec((seq, RET_DK), tokblk),
        pl.BlockSpec((seq, RET_DV), tokblk),
        pl.BlockSpec((seq, RET_DV), tokblk),
        pl.BlockSpec((1, RET_DV), lambda b, h: (0, h)),
        pl.BlockSpec((1, 1, LANE), lambda b, h: (h, 0, 0)),
        pl.BlockSpec((1, 1, LANE), lambda b, h: (h, 0, 0)),
    ]
    args = [rq, rk, rv, srg, gn, lf, lb]
    if has_state:
        in_specs += [pl.BlockSpec((1, 1, RET_DK, RET_DV), st)] * 2
        args += [s0f, s0b]
    st_shape = jax.ShapeDtypeStruct((batch, RET_HEADS, RET_DK, RET_DV), F32)
    return pl.pallas_call(
        functools.partial(_ret_kernel, has_state, seq),
        grid=(batch, RET_HEADS),
        in_specs=in_specs,
        out_specs=[
            pl.BlockSpec((seq, RET_DV), lambda b, h: (b, h)),
            pl.BlockSpec((1, 1, RET_DK, RET_DV), st),
            pl.BlockSpec((1, 1, RET_DK, RET_DV), st),
        ],
        out_shape=[jax.ShapeDtypeStruct((batch * seq, RET_HEADS * RET_DV), BF16), st_shape, st_shape],
        scratch_shapes=[pltpu.VMEM((seq, RET_DV), F32), pltpu.VMEM((seq, RET_DV), F32)],
        compiler_params=pltpu.CompilerParams(
            dimension_semantics=("parallel", "parallel"), vmem_limit_bytes=VMEM_LIMIT),
        name="ret_b%d" % batch,
    )(*args)


def _merge_kernel(np_blocks, ap_ref, as_ref, rp_ref, rs_ref, sga_ref, sgb_ref, xp_ref, xs_ref, mod_ref, g_ref,
                  wa_ref, wb_ref, wo_ref, x1_ref, h2_ref):
    is_prompt = pl.program_id(0) < np_blocks
    attn = jnp.where(is_prompt, ap_ref[...], as_ref[...])
    ret = jnp.where(is_prompt, rp_ref[...], rs_ref[...])
    x = jnp.where(is_prompt, xp_ref[...], xs_ref[...])
    m = sga_ref[...] * _dot(attn, wa_ref[...]) + sgb_ref[...] * _dot(ret, wb_ref[...])
    o = _dot(m.astype(BF16), wo_ref[...])
    x1 = x + mod_ref[0, 2:3, :] * o
    x1_ref[...] = x1
    h2 = _rms(x1, g_ref[...]) * (1.0 + mod_ref[0, 4:5, :]) + mod_ref[0, 3:4, :]
    h2_ref[...] = h2.astype(BF16)


def _merge_call(attn_p, attn_s, ret_p, ret_s, sga, sgb, xp, xs, mod3, norm_ffn, wa, wb, wo, np_blocks, seq_blocks):
    n = xp.shape[0] + xs.shape[0]
    nblk = n // TM
    const = lambda i: (0, 0)
    row = lambda i: (i, 0)
    p_idx = lambda i: (jnp.minimum(i, np_blocks - 1), 0)
    s_idx = lambda i: (jnp.maximum(i - np_blocks, 0), 0)

    def mod_idx(i):
        return (jnp.where(i < np_blocks, 0, 1 + (i - np_blocks) // seq_blocks), 0, 0)

    w = attn_p.shape[1]
    return pl.pallas_call(
        functools.partial(_merge_kernel, np_blocks),
        grid=(nblk,),
        in_specs=[
            pl.BlockSpec((TM, w), p_idx),
            pl.BlockSpec((TM, w), s_idx),
            pl.BlockSpec((TM, w), p_idx),
            pl.BlockSpec((TM, w), s_idx),
            pl.BlockSpec((TM, D_MODEL), row),
            pl.BlockSpec((TM, D_MODEL), row),
            pl.BlockSpec((TM, D_MODEL), p_idx),
            pl.BlockSpec((TM, D_MODEL), s_idx),
            pl.BlockSpec((1, 8, D_MODEL), mod_idx),
            pl.BlockSpec((1, D_MODEL), const),
            pl.BlockSpec(wa.shape, const),
            pl.BlockSpec(wb.shape, const),
            pl.BlockSpec(wo.shape, const),
        ],
        out_specs=[pl.BlockSpec((TM, D_MODEL), row), pl.BlockSpec((TM, D_MODEL), row)],
        out_shape=[jax.ShapeDtypeStruct((n, D_MODEL), F32), jax.ShapeDtypeStruct((n, D_MODEL), BF16)],
        compiler_params=pltpu.CompilerParams(dimension_semantics=("parallel",), vmem_limit_bytes=VMEM_LIMIT),
        name="merge",
    )(attn_p, attn_s, ret_p, ret_s, sga, sgb, xp, xs, mod3, norm_ffn, wa, wb, wo)


def _top_rows(s, k):
    rows = []
    for _ in range(k):
        m = jnp.max(s, axis=0, keepdims=True)
        rows.append(m)
        s = jnp.where(s == m, -jnp.inf, s)
    return rows


def _sort_pairs(n):
    pairs = []
    p = 1
    while p < n:
        k = p
        while k >= 1:
            for j in range(k % p, n - k, 2 * k):
                for i in range(min(k, n - j - k)):
                    if (i + j) // (2 * p) == (i + j + k) // (2 * p):
                        pairs.append((i + j, i + j + k))
            k //= 2
        p *= 2
    return pairs


def _exchange(vs, i, j):
    vs[i], vs[j] = jnp.maximum(vs[i], vs[j]), jnp.minimum(vs[i], vs[j])


def _top_sorted(blocks):
    k = len(blocks)
    vs = list(blocks)
    for i, j in _sort_pairs(k):
        _exchange(vs, i, j)
    shift = SUBLANES // 2
    while shift >= 1:
        other = [pltpu.roll(v, shift, 0) for v in vs]
        vs = [jnp.maximum(vs[i], other[k - 1 - i]) for i in range(k)]
        d = k // 2
        while d >= 1:
            for i in range(k):
                if i & d == 0:
                    _exchange(vs, i, i + d)
            d //= 2
        shift //= 2
    return vs


def _stack_rows(rows):
    n, cols = len(rows), rows[0].shape[1]
    ridx = lax.broadcasted_iota(jnp.int32, (n, cols), 0)
    out = jnp.broadcast_to(rows[0], (n, cols))
    for a in range(1, n):
        out = jnp.where(ridx == a, rows[a], out)
    return out


def _route_tile(s1, s2):
    nb = N_KEYS // SUBLANES
    assert nb == PEER_TOPK
    b1 = [s1[SUBLANES * i:SUBLANES * (i + 1), :] for i in range(nb)]
    b2 = [s2[SUBLANES * i:SUBLANES * (i + 1), :] for i in range(nb)]
    t1 = _top_sorted(b1)
    t2 = _top_sorted(b2)
    r1 = [t[0:1, :] for t in t1]
    r2 = [t[0:1, :] for t in t2]
    v1 = _stack_rows(r1)
    v2 = _stack_rows(r2)
    k8 = PEER_TOPK // 2
    cands = [r1[0] + v2, r1[1] + v2[0:k8, :], r1[2] + v2[0:k8, :], r1[3] + v2[0:k8, :],
             v1 + r2[0], v1[0:k8, :] + r2[1], v1[0:k8, :] + r2[2]]
    tau = _top_rows(jnp.concatenate(cands, axis=0), PEER_TOPK)[-1]
    e1 = [jnp.exp(r - r1[0]) for r in r1]
    e2v = jnp.exp(v2 - r2[0])
    cnts, zs = [], []
    for a in range(PEER_TOPK):
        sel = (r1[a] + v2) >= tau
        cnts.append(jnp.sum(jnp.where(sel, 1.0, 0.0), axis=0, keepdims=True))
        zs.append(jnp.sum(jnp.where(sel, e2v, 0.0), axis=0, keepdims=True))
    z = e1[0] * zs[0]
    for a in range(1, PEER_TOPK):
        z = z + e1[a] * zs[a]
    zinv = 1.0 / z
    cnts = [jnp.broadcast_to(c, b1[0].shape) for c in cnts]
    rank2, cnt = [], []
    for i in range(nb):
        rk = jnp.zeros(b2[i].shape, F32)
        ct = jnp.zeros(b1[i].shape, F32)
        for a in range(PEER_TOPK):
            rk = jnp.where(t2[a] > b2[i], float(a + 1), rk)
            ct = jnp.where(b1[i] == t1[a], cnts[a], ct)
        rank2.append(rk)
        cnt.append(ct)
    rank2 = jnp.concatenate(rank2, axis=0)
    cnt = jnp.concatenate(cnt, axis=0)
    e2 = jnp.exp(s2 - r2[0])
    cc = jnp.exp(s1 - r1[0]) * zinv
    return rank2, e2, cnt, cc


def _peer_route(qt_ref, k1_ref, k2_ref, r2_ref, e2_ref, cnt_ref, cc_ref):
    half = PEER_DQ // 2
    tm = qt_ref.shape[1]

    def head(hd, carry):
        q0 = pl.multiple_of(hd * PEER_DQ, PEER_DQ)
        k0 = pl.multiple_of(hd * N_KEYS, N_KEYS)
        k1 = k1_ref[pl.ds(k0, N_KEYS), :]
        k2 = k2_ref[pl.ds(k0, N_KEYS), :]
        for st in range(tm // LANE):
            lanes = slice(st * LANE, (st + 1) * LANE)
            s1 = _dot(k1, qt_ref[pl.ds(q0, half), lanes].astype(BF16))
            s2 = _dot(k2, qt_ref[pl.ds(q0 + half, half), lanes].astype(BF16))
            rank2, e2, cnt, cc = _route_tile(s1, s2)
            r2_ref[pl.ds(k0, N_KEYS), lanes] = rank2.astype(BF16)
            e2_ref[pl.ds(k0, N_KEYS), lanes] = e2.astype(BF16)
            cnt_ref[pl.ds(k0, N_KEYS), lanes] = cnt
            cc_ref[pl.ds(k0, N_KEYS), lanes] = cc
        return carry

    lax.fori_loop(0, PEER_HEADS, head, 0)


def _peer_kernel(h2_ref, x1_ref, mod_ref, g_ref, wqt_ref, k1_ref, k2_ref, u_ref, vt_ref,
                 y_ref, qt_ref, r2_ref, e2_ref, cnt_ref, cc_ref, acc_ref):
    j = pl.program_id(1)
    nj = pl.num_programs(1)
    tm = h2_ref.shape[0]

    @pl.when(j == 0)
    def _():
        qt_ref[...] = _dot_nt(wqt_ref[...], h2_ref[...])
        _peer_route(qt_ref, k1_ref, k2_ref, r2_ref, e2_ref, cnt_ref, cc_ref)
        acc_ref[...] = jnp.zeros(acc_ref.shape, F32)

    parts = []
    for il in range(E_CHUNK // N_KEYS):
        if il % PEER_KEYS_PER_DOT == 0:
            rows = slice(il * N_KEYS, (il + PEER_KEYS_PER_DOT) * N_KEYS)
            hu = _dot_nt(h2_ref[...], u_ref[rows, :])
            act = jax.nn.gelu(hu.T.astype(BF16), approximate=True)
        g = None
        for hd in range(PEER_HEADS):
            r = hd * N_KEYS + j * (E_CHUNK // N_KEYS) + il
            cnt = jnp.broadcast_to(cnt_ref[pl.ds(r, 1), :], (N_KEYS, tm)).astype(BF16)
            cc = jnp.broadcast_to(cc_ref[pl.ds(r, 1), :], (N_KEYS, tm)).astype(BF16)
            kr = slice(hd * N_KEYS, (hd + 1) * N_KEYS)
            e2 = e2_ref[kr, :]
            t = jnp.where(r2_ref[kr, :] < cnt, e2, jnp.zeros_like(e2)) * cc
            g = t if g is None else g + t
        ia = il % PEER_KEYS_PER_DOT
        parts.append(g * act[ia * N_KEYS:(ia + 1) * N_KEYS, :])
    p = jnp.concatenate(parts, axis=0)
    acc_ref[...] += _dot(vt_ref[...], p)

    @pl.when(j == nj - 1)
    def _():
        x2 = x1_ref[...] + mod_ref[0, 5:6, :] * acc_ref[...].T
        y_ref[...] = _rms(x2, g_ref[...])


def _peer_call(h2, x1, mod3, norm_final, wqt, k1, k2, u_b, vt_b, tok_off, n, seq):
    tm = TM_PEER
    nblk = n // tm
    off = tok_off // tm
    ne = u_b.shape[0] // E_CHUNK
    nq = PEER_HEADS * PEER_DQ
    nk = PEER_HEADS * N_KEYS

    def mod_idx(i, j):
        return (0 if seq is None else 1 + i // (seq // tm), 0, 0)

    const = lambda i, j: (0, 0)
    tok = lambda i, j: (off + i, 0)
    return pl.pallas_call(
        _peer_kernel,
        grid=(nblk, ne),
        in_specs=[
            pl.BlockSpec((tm, D_MODEL), tok),
            pl.BlockSpec((tm, D_MODEL), tok),
            pl.BlockSpec((1, 8, D_MODEL), mod_idx),
            pl.BlockSpec((1, D_MODEL), const),
            pl.BlockSpec(wqt.shape, const),
            pl.BlockSpec(k1.shape, const),
            pl.BlockSpec(k2.shape, const),
            pl.BlockSpec((E_CHUNK, D_MODEL), lambda i, j: (j, 0)),
            pl.BlockSpec((D_MODEL, E_CHUNK), lambda i, j: (0, j)),
        ],
        out_specs=pl.BlockSpec((tm, D_MODEL), lambda i, j: (i, 0)),
        out_shape=jax.ShapeDtypeStruct((n, D_MODEL), F32),
        scratch_shapes=[
            pltpu.VMEM((nq, tm), F32),
            pltpu.VMEM((nk, tm), BF16),
            pltpu.VMEM((nk, tm), BF16),
            pltpu.VMEM((nk, tm), F32),
            pltpu.VMEM((nk, tm), F32),
            pltpu.VMEM((D_MODEL, tm), F32),
        ],
        compiler_params=pltpu.CompilerParams(
            dimension_semantics=("parallel", "arbitrary"), vmem_limit_bytes=VMEM_LIMIT),
        name="peer_n%d" % n,
    )(h2, x1, mod3, norm_final, wqt, k1, k2, u_b, vt_b)


def _rope_tables(seq):
    rows = seq // GRID_W
    row = jnp.repeat(jnp.arange(rows, dtype=F32), GRID_W)
    col = jnp.tile(jnp.arange(GRID_W, dtype=F32), rows)
    nf = MLA_ROPE // 4
    freqs = jnp.power(ROPE_BASE, -jnp.arange(nf, dtype=F32) / nf)
    ang = jnp.concatenate([row[:, None] * freqs, col[:, None] * freqs], axis=-1)
    cos, sin = jnp.cos(ang), jnp.sin(ang)
    pad = jnp.zeros((seq, LANE - MLA_ROPE), F32)
    ctab = jnp.concatenate([cos, cos, pad], axis=1)
    stab = jnp.concatenate([-sin, sin, pad], axis=1)
    cid = jnp.concatenate([jnp.ones((TM, MLA_ROPE), F32), jnp.zeros((TM, LANE - MLA_ROPE), F32)], axis=1)
    ctab = jnp.concatenate([ctab, cid], axis=0)
    stab = jnp.concatenate([stab, jnp.zeros((TM, LANE), F32)], axis=0)
    return ctab, stab


def _pad_cols(a, width):
    return jnp.concatenate([a, jnp.zeros((a.shape[0], width - a.shape[1]), a.dtype)], axis=1)


def kernel(x_prompt, x_sample, c, cache_ckv, cache_krope, state_ret_fwd, state_ret_bwd, c_ctx, w_mod, b_mod, norm_mix, norm_ffn, norm_final, w_in, q_norm, kv_norm, w_uq, w_ukv, ret_logit_fwd, ret_logit_bwd, ret_gn, w_up_a, w_up_b, w_o, peer_wq, peer_keys1, peer_keys2, peer_u, peer_v):
    depth = w_mod.shape[0]
    assert depth == 1
    bp, sp, _ = x_prompt.shape
    bs, ss, _ = x_sample.shape
    past = cache_ckv.shape[2]
    n_p, n_s = bp * sp, bs * ss
    assert 1 + bs <= MOD_ROWS and sp == TM and ss % TM == 0 and ss % TM_PEER == 0 and n_p % TM_PEER == 0
    np_blocks, seq_blocks = n_p // TM, ss // TM
    l = 0

    half = MLA_ROPE // 2
    kr1 = w_in[l][:, 384:384 + half]
    kr2 = w_in[l][:, 384 + half:384 + MLA_ROPE]
    win_p = jnp.concatenate([
        w_in[l][:, :384],
        _pad_cols(jnp.concatenate([kr1, kr2], axis=1), LANE),
        _pad_cols(jnp.concatenate([kr2, kr1], axis=1), LANE),
        w_in[l][:, 384 + MLA_ROPE:],
    ], axis=1).astype(BF16)
    assert win_p.shape[1] == _D_IN_P
    w3 = w_uq[l].reshape(Q_LORA, MLA_HEADS, MLA_NOPE + MLA_ROPE)
    r1 = w3[:, :, MLA_NOPE:MLA_NOPE + half]
    r2 = w3[:, :, MLA_NOPE + half:]
    zpad = jnp.zeros((Q_LORA, MLA_HEADS, LANE - MLA_ROPE), F32)
    wuq_all = jnp.concatenate([
        w3[:, :, :MLA_NOPE].reshape(Q_LORA, MLA_HEADS * MLA_NOPE),
        jnp.concatenate([r1, r2, zpad], axis=2).reshape(Q_LORA, MLA_HEADS * LANE),
        jnp.concatenate([r2, r1, zpad], axis=2).reshape(Q_LORA, MLA_HEADS * LANE),
    ], axis=1).astype(BF16)
    wkv3 = w_ukv[l].reshape(KV_LORA, MLA_HEADS, MLA_NOPE + MLA_V)
    eye = jnp.eye(MLA_HEADS, dtype=F32)
    wk_hdl = jnp.transpose(wkv3[:, :, :MLA_NOPE], (1, 2, 0))
    wv_hld = jnp.transpose(wkv3[:, :, MLA_NOPE:], (1, 0, 2))
    bdk = (wk_hdl[:, :, None, :] * eye[:, None, :, None]).reshape(
        MLA_HEADS * MLA_NOPE, MLA_HEADS * KV_LORA).astype(BF16)
    bdv = (wv_hld[:, :, None, :] * eye[:, None, :, None]).reshape(
        MLA_HEADS * KV_LORA, MLA_HEADS * MLA_V).astype(BF16)
    wa = w_up_a[l].astype(BF16)
    wb = w_up_b[l].astype(BF16)
    wo = w_o[l].astype(BF16)
    wqt = peer_wq[l].T.astype(BF16)
    k1 = peer_keys1[l].reshape(PEER_HEADS * N_KEYS, PEER_DQ // 2).astype(BF16)
    k2 = peer_keys2[l].reshape(PEER_HEADS * N_KEYS, PEER_DQ // 2).astype(BF16)
    u_b = peer_u[l].astype(BF16)
    vt_b = peer_v[l].T.astype(BF16)
    ctab, stab = _rope_tables(ss)
    lf = jnp.broadcast_to(ret_logit_fwd[l][:, None, None], (RET_HEADS, 1, LANE))
    lb = jnp.broadcast_to(ret_logit_bwd[l][:, None, None], (RET_HEADS, 1, LANE))

    cc = jnp.concatenate([c_ctx[None, :], c, jnp.zeros((MOD_ROWS - 1 - bs, D_MODEL), F32)], axis=0)
    mod = _mod_call(cc, w_mod[l], b_mod[l][None, :])
    mod3 = jnp.concatenate([mod.reshape(MOD_ROWS, 6, D_MODEL), jnp.zeros((MOD_ROWS, 2, D_MODEL), F32)], axis=1)

    xp = x_prompt.reshape(n_p, D_MODEL)
    xs = x_sample.reshape(n_s, D_MODEL)
    qt, kk, ckvt, ckv32, kr32, rq, rk, rv, srg, sga, sgb = _proj_call(
        xp, xs, mod3, norm_mix[l][None, :], win_p, q_norm[l][None, :], kv_norm[l][None, :], wuq_all.T, bdk.T,
        ctab, stab, np_blocks, seq_blocks)

    kk_p = kk[:n_p].reshape(bp, sp, 2 * LANE)
    vt_p = jnp.transpose(ckvt[:, :n_p].reshape(KV_LORA, bp, sp), (1, 0, 2))
    cache_kk = jnp.concatenate([cache_ckv[:, l], _pad_cols(
        cache_krope[:, l].reshape(bs * past, MLA_ROPE), LANE).reshape(bs, past, LANE)], axis=2).astype(BF16)
    kk_s = jnp.concatenate([kk[n_p:].reshape(bs, ss, 2 * LANE), cache_kk], axis=1)
    vt_s = jnp.concatenate([jnp.transpose(ckvt[:, n_p:].reshape(KV_LORA, bs, ss), (1, 0, 2)),
                            jnp.transpose(cache_ckv[:, l], (0, 2, 1)).astype(BF16)], axis=2)
    attn_p = _attn_call(qt, kk_p, vt_p, bdv, 0, bp, sp, sp)
    attn_s = _attn_call(qt, kk_s, vt_s, bdv, n_p, bs, ss, 512)

    gn = ret_gn[l][None, :]
    ret_p, sf, sb = _ret_call(rq, rk, rv, srg, gn, lf, lb, None, None, 0, bp, sp)
    ret_s, _, _ = _ret_call(rq, rk, rv, srg, gn, lf, lb, state_ret_fwd[:, l], state_ret_bwd[:, l], n_p, bs, ss)

    x1, h2 = _merge_call(attn_p, attn_s, ret_p, ret_s, sga, sgb, xp, xs, mod3, norm_ffn[l][None, :],
                          wa, wb, wo, np_blocks, seq_blocks)

    nf = norm_final[None, :]
    y_prompt = _peer_call(h2, x1, mod3, nf, wqt, k1, k2, u_b, vt_b, 0, n_p, None).reshape(bp, sp, D_MODEL)
    y_sample = _peer_call(h2, x1, mod3, nf, wqt, k1, k2, u_b, vt_b, n_p, n_s, ss).reshape(bs, ss, D_MODEL)
    new_ckv = ckv32[:n_p].reshape(bp, 1, sp, KV_LORA)
    new_kr = kr32[:n_p, :MLA_ROPE].reshape(bp, 1, sp, MLA_ROPE)
    return (y_prompt, y_sample, new_ckv, new_kr, sf[:, None], sb[:, None])
```

```python
import functools

import jax
import jax.numpy as jnp
from jax import lax
from jax.experimental import pallas as pl
from jax.experimental.pallas import tpu as pltpu

F32 = jnp.float32
BF16 = jnp.bfloat16

D_MODEL = 1024
GRID_W = 64
EPS = 1e-6
MLA_HEADS = 8
MLA_NOPE = 64
MLA_ROPE = 32
MLA_V = 64
Q_LORA = 256
KV_LORA = 128
ROPE_BASE = 10000.0
RET_HEADS = 4
RET_DK = 128
RET_DV = 128
PEER_HEADS = 8
PEER_DQ = 256
N_KEYS = 128
PEER_TOPK = 16

LANE = 128
SUBLANES = 8
LOG2_E = 1.4426950408889634
VMEM_LIMIT = 56 * 1024 * 1024

TM = 256
TQ = 128
RET_CHUNK = 256
RET_HEADS_PER_STEP = 4
TM_PEER = 512
E_CHUNK = 2048
PEER_KEYS_PER_DOT = 2
MOD_ROWS = 16

_C_CQ = 0
_C_CKV = 256
_C_KR = 384
_C_KRS = 512
_C_RQ = 640
_C_RK = 1152
_C_RV = 1664
_C_RG = 2176
_C_GA = 2688
_C_GB = 3712
_D_IN_P = 4736


def _rms(x, g):
    return x * lax.rsqrt(jnp.mean(x * x, axis=-1, keepdims=True) + EPS) * g


def _dot(a, b):
    return jnp.dot(a, b, preferred_element_type=F32)


def _dot_nt(a, b):
    return lax.dot_general(a, b, (((1,), (1,)), ((), ())), preferred_element_type=F32)


def _mod_kernel(c_ref, w_ref, b_ref, o_ref):
    c = c_ref[...]
    s = c * jax.nn.sigmoid(c)
    o_ref[...] = _dot(s.astype(BF16), w_ref[...].astype(BF16)) + b_ref[...]


def _mod_call(cc, w_mod, b_mod):
    n = w_mod.shape[1]
    bn = 1024
    return pl.pallas_call(
        _mod_kernel,
        grid=(n // bn,),
        in_specs=[
            pl.BlockSpec((MOD_ROWS, D_MODEL), lambda j: (0, 0)),
            pl.BlockSpec((D_MODEL, bn), lambda j: (0, j)),
            pl.BlockSpec((1, bn), lambda j: (0, j)),
        ],
        out_specs=pl.BlockSpec((MOD_ROWS, bn), lambda j: (0, j)),
        out_shape=jax.ShapeDtypeStruct((MOD_ROWS, n), F32),
        compiler_params=pltpu.CompilerParams(dimension_semantics=("parallel",), vmem_limit_bytes=VMEM_LIMIT),
        name="mod",
    )(cc, w_mod, b_mod)


def _proj_kernel(np_blocks, xp_ref, xs_ref, mod_ref, g_ref, win_ref, qn_ref, kvn_ref, wuqt_ref, bdkt_ref,
                 ct_ref, st_ref, ctt_ref, stt_ref,
                 qt_ref, kk_ref, ckvt_ref, ckv_ref, kr_ref, rq_ref, rk_ref, rv_ref, srg_ref, sga_ref, sgb_ref):
    x = jnp.where(pl.program_id(0) < np_blocks, xp_ref[...], xs_ref[...])
    h = _rms(x, g_ref[...]) * (1.0 + mod_ref[0, 1:2, :]) + mod_ref[0, 0:1, :]
    y = _dot(h.astype(BF16), win_ref[...])

    ct = ct_ref[...]
    st = st_ref[...]
    ckv = _rms(y[:, _C_CKV:_C_CKV + KV_LORA], kvn_ref[...])
    kr = y[:, _C_KR:_C_KR + LANE]
    krot = kr * ct + y[:, _C_KRS:_C_KRS + LANE] * st
    ckv_ref[...] = ckv
    kr_ref[...] = kr
    kk_ref[:, 0:KV_LORA] = ckv.astype(BF16)
    kk_ref[:, KV_LORA:2 * KV_LORA] = krot.astype(BF16)
    ckvt_ref[...] = ckv.T.astype(BF16)

    cq = _rms(y[:, _C_CQ:_C_CQ + Q_LORA], qn_ref[...])
    qat = _dot(wuqt_ref[...], cq.T.astype(BF16))
    nn = MLA_HEADS * MLA_NOPE
    nr = MLA_HEADS * LANE
    qlat = _dot(bdkt_ref[...], qat[:nn, :].astype(BF16))
    ctt = ctt_ref[...]
    stt = stt_ref[...]
    scale = (MLA_NOPE + MLA_ROPE) ** -0.5 * LOG2_E
    for hd in range(MLA_HEADS):
        lo = hd * LANE
        qrot = qat[nn + lo:nn + lo + LANE, :] * ctt + qat[nn + nr + lo:nn + nr + lo + LANE, :] * stt
        qt_ref[2 * lo:2 * lo + LANE, :] = (qlat[lo:lo + LANE, :] * scale).astype(BF16)
        qt_ref[2 * lo + LANE:2 * lo + 2 * LANE, :] = (qrot * scale).astype(BF16)

    hk = RET_HEADS * RET_DK
    rq_ref[...] = y[:, _C_RQ:_C_RQ + hk].astype(BF16)
    rk_ref[...] = (y[:, _C_RK:_C_RK + hk] * (RET_DK ** -0.5)).astype(BF16)
    rv_ref[...] = y[:, _C_RV:_C_RV + hk].astype(BF16)
    rg = y[:, _C_RG:_C_RG + hk]
    srg_ref[...] = (rg * jax.nn.sigmoid(rg)).astype(BF16)
    sga_ref[...] = jax.nn.sigmoid(y[:, _C_GA:_C_GA + D_MODEL]).astype(BF16)
    sgb_ref[...] = jax.nn.sigmoid(y[:, _C_GB:_C_GB + D_MODEL]).astype(BF16)


def _proj_call(xp, xs, mod3, norm_mix, win_p, q_norm, kv_norm, wuq_t, bdk_t, ctab, stab, np_blocks, seq_blocks):
    n = xp.shape[0] + xs.shape[0]
    nblk = n // TM
    rope_id_block = ctab.shape[0] // TM - 1

    def mod_idx(i):
        return (jnp.where(i < np_blocks, 0, 1 + (i - np_blocks) // seq_blocks), 0, 0)

    def rope_blk(i):
        return jnp.where(i < np_blocks, rope_id_block, (i - np_blocks) % seq_blocks)

    rope_idx = lambda i: (rope_blk(i), 0)
    rope_idx_t = lambda i: (0, rope_blk(i))
    const = lambda i: (0, 0)
    row = lambda i: (i, 0)
    col = lambda i: (0, i)
    hk = RET_HEADS * RET_DK
    out_shape = [
        jax.ShapeDtypeStruct((2 * MLA_HEADS * LANE, n), BF16),
        jax.ShapeDtypeStruct((n, 2 * LANE), BF16),
        jax.ShapeDtypeStruct((KV_LORA, n), BF16),
        jax.ShapeDtypeStruct((n, KV_LORA), F32),
        jax.ShapeDtypeStruct((n, LANE), F32),
        jax.ShapeDtypeStruct((n, hk), BF16),
        jax.ShapeDtypeStruct((n, hk), BF16),
        jax.ShapeDtypeStruct((n, hk), BF16),
        jax.ShapeDtypeStruct((n, hk), BF16),
        jax.ShapeDtypeStruct((n, D_MODEL), BF16),
        jax.ShapeDtypeStruct((n, D_MODEL), BF16),
    ]
    out_specs = [pl.BlockSpec((s.shape[0], TM), col) if s.shape[1] == n else pl.BlockSpec((TM, s.shape[1]), row)
                 for s in out_shape]
    return pl.pallas_call(
        functools.partial(_proj_kernel, np_blocks),
        grid=(nblk,),
        in_specs=[
            pl.BlockSpec((TM, D_MODEL), lambda i: (jnp.minimum(i, np_blocks - 1), 0)),
            pl.BlockSpec((TM, D_MODEL), lambda i: (jnp.maximum(i - np_blocks, 0), 0)),
            pl.BlockSpec((1, 8, D_MODEL), mod_idx),
            pl.BlockSpec((1, D_MODEL), const),
            pl.BlockSpec(win_p.shape, const),
            pl.BlockSpec((1, Q_LORA), const),
            pl.BlockSpec((1, KV_LORA), const),
            pl.BlockSpec(wuq_t.shape, const),
            pl.BlockSpec(bdk_t.shape, const),
            pl.BlockSpec((TM, LANE), rope_idx),
            pl.BlockSpec((TM, LANE), rope_idx),
            pl.BlockSpec((LANE, TM), rope_idx_t),
            pl.BlockSpec((LANE, TM), rope_idx_t),
        ],
        out_specs=out_specs,
        out_shape=out_shape,
        compiler_params=pltpu.CompilerParams(dimension_semantics=("parallel",), vmem_limit_bytes=VMEM_LIMIT),
        name="proj",
    )(xp, xs, mod3, norm_mix, win_p, q_norm, kv_norm, wuq_t, bdk_t, ctab, stab, ctab.T, stab.T)


def _attn_kernel(nkb, tkb, qt_ref, k_ref, vt_ref, bdv_ref, o_ref, s_ref):
    fw = 2 * LANE
    qt = jnp.concatenate([qt_ref[hd * fw:(hd + 1) * fw, :] for hd in range(MLA_HEADS)], axis=1)
    m = None
    for kb in range(nkb):
        s = _dot(k_ref[0, kb * tkb:(kb + 1) * tkb, :], qt)
        s_ref[kb * tkb:(kb + 1) * tkb, :] = s
        mk = jnp.max(s, axis=0, keepdims=True)
        m = mk if m is None else jnp.maximum(m, mk)
    l = None
    acc = None
    for kb in range(nkb):
        p = jnp.exp2(s_ref[kb * tkb:(kb + 1) * tkb, :] - m)
        lk = jnp.sum(p, axis=0, keepdims=True)
        ak = _dot(vt_ref[0, :, kb * tkb:(kb + 1) * tkb], p.astype(BF16))
        l = lk if l is None else l + lk
        acc = ak if acc is None else acc + ak
    o = acc / l
    ocat = jnp.concatenate([o[:, hd * TQ:(hd + 1) * TQ].T for hd in range(MLA_HEADS)], axis=1)
    o_ref[...] = _dot(ocat.astype(BF16), bdv_ref[...]).astype(BF16)


def _attn_call(qt, kk3, vt3, bdv, tok_off, batch, seq, tkb):
    nq = seq // TQ
    tk = kk3.shape[1]
    off = tok_off // TQ
    return pl.pallas_call(
        functools.partial(_attn_kernel, tk // tkb, tkb),
        grid=(batch, nq),
        in_specs=[
            pl.BlockSpec((qt.shape[0], TQ), lambda b, qi: (0, off + b * nq + qi)),
            pl.BlockSpec((1, tk, 2 * LANE), lambda b, qi: (b, 0, 0)),
            pl.BlockSpec((1, KV_LORA, tk), lambda b, qi: (b, 0, 0)),
            pl.BlockSpec(bdv.shape, lambda b, qi: (0, 0)),
        ],
        out_specs=pl.BlockSpec((TQ, MLA_HEADS * MLA_V), lambda b, qi: (b * nq + qi, 0)),
        out_shape=jax.ShapeDtypeStruct((batch * seq, MLA_HEADS * MLA_V), BF16),
        scratch_shapes=[pltpu.VMEM((tk, MLA_HEADS * TQ), F32)],
        compiler_params=pltpu.CompilerParams(
            dimension_semantics=("parallel", "parallel"), vmem_limit_bytes=VMEM_LIMIT),
        name="attn_b%d" % batch,
    )(qt, kk3, vt3, bdv)


def _log_sigmoid(x):
    return jnp.minimum(x, 0.0) - jnp.log(1.0 + jnp.exp(-jnp.abs(x)))


def _ret_kernel(has_state, seq, *refs):
    if has_state:
        (q_ref, k_ref, v_ref, srg_ref, gn_ref, lf_ref, lb_ref, s0f_ref, s0b_ref,
         o_ref, sf_ref, sb_ref, yf_ref, yb_ref) = refs
    else:
        (q_ref, k_ref, v_ref, srg_ref, gn_ref, lf_ref, lb_ref,
         o_ref, sf_ref, sb_ref, yf_ref, yb_ref) = refs
    c = RET_CHUNK
    n = seq // c
    ii = lax.broadcasted_iota(jnp.int32, (c, c), 0)
    jj = lax.broadcasted_iota(jnp.int32, (c, c), 1)
    diff = (ii - jj).astype(F32)
    idx = lax.broadcasted_iota(jnp.int32, (c, 1), 0).astype(F32)

    consts = []
    for hh in range(RET_HEADS_PER_STEP):
        lgf = _log_sigmoid(lf_ref[hh, :, 0:1])
        lgb = _log_sigmoid(lb_ref[hh, :, 0:1])
        fwd = (jnp.where(diff >= 0, jnp.exp(lgf * jnp.maximum(diff, 0.0)), 0.0),
               jnp.exp(lgf * (idx + 1.0)), jnp.exp(lgf * (c - 1.0 - idx)), jnp.exp(lgf * c))
        bwd = (jnp.where(diff <= 0, jnp.exp(lgb * jnp.maximum(-diff, 0.0)), 0.0),
               jnp.exp(lgb * (c - idx)), jnp.exp(lgb * idx), jnp.exp(lgb * c))
        consts.append((fwd, bwd))

    def chunk(hh, start, state, dec, y_ref):
        dmat, qdec, kdec, cdec = dec
        lanes = slice(hh * RET_DK, (hh + 1) * RET_DK)
        q = q_ref[pl.ds(start, c), lanes]
        k = k_ref[pl.ds(start, c), lanes]
        v = v_ref[pl.ds(start, c), lanes]
        sc = _dot_nt(q, k) * dmat
        y = _dot(sc.astype(BF16), v) + _dot(q, state.astype(BF16)) * qdec
        y_ref[pl.ds(start, c), lanes] = y
        kdt = (k.astype(F32) * kdec).T.astype(BF16)
        return state * cdec + _dot(kdt, v)

    def body(t, carry):
        out = []
        for hh in range(RET_HEADS_PER_STEP):
            sf, sb = carry[hh]
            sf = chunk(hh, pl.multiple_of(t * c, c), sf, consts[hh][0], yf_ref)
            sb = chunk(hh, pl.multiple_of((n - 1 - t) * c, c), sb, consts[hh][1], yb_ref)
            out.append((sf, sb))
        return tuple(out)

    if has_state:
        init = tuple((s0f_ref[0, hh], s0b_ref[0, hh]) for hh in range(RET_HEADS_PER_STEP))
    else:
        zero = jnp.zeros((RET_DK, RET_DV), F32)
        init = tuple((zero, zero) for _ in range(RET_HEADS_PER_STEP))
    final = lax.fori_loop(0, n, body, init)
    for hh in range(RET_HEADS_PER_STEP):
        sf_ref[0, hh] = final[hh][0]
        sb_ref[0, hh] = final[hh][1]

    for hh in range(RET_HEADS_PER_STEP):
        lanes = slice(hh * RET_DV, (hh + 1) * RET_DV)
        y = yf_ref[:, lanes] + yb_ref[:, lanes]
        mu = jnp.mean(y, axis=-1, keepdims=True)
        yc = y - mu
        var = jnp.mean(yc * yc, axis=-1, keepdims=True)
        yn = yc * lax.rsqrt(var + EPS)
        o_ref[:, lanes] = (srg_ref[:, lanes] * (yn * gn_ref[:, lanes])).astype(BF16)


def _ret_call(rq, rk, rv, srg, gn, lf, lb, s0f, s0b, tok_off, batch, seq):
    off = tok_off // seq
    has_state = s0f is not None
    hp = RET_HEADS_PER_STEP
    tokblk = lambda b, h: (off + b, h)
    st = lambda b, h: (b, h, 0, 0)
    in_specs = [
        pl.BlockSpec((seq, hp * RET_DK), tokblk),
        pl.BlockSpec((seq, hp * RET_DK), tokblk),
        pl.BlockSpec((seq, hp * RET_DV), tokblk),
        pl.BlockSpec((seq, hp * RET_DV), tokblk),
        pl.BlockSpec((1, hp * RET_DV), lambda b, h: (0, h)),
        pl.BlockSpec((hp, 1, LANE), lambda b, h: (h, 0, 0)),
        pl.BlockSpec((hp, 1, LANE), lambda b, h: (h, 0, 0)),
    ]
    args = [rq, rk, rv, srg, gn, lf, lb]
    if has_state:
        in_specs += [pl.BlockSpec((1, hp, RET_DK, RET_DV), st)] * 2
        args += [s0f, s0b]
    st_shape = jax.ShapeDtypeStruct((batch, RET_HEADS, RET_DK, RET_DV), F32)
    return pl.pallas_call(
        functools.partial(_ret_kernel, has_state, seq),
        grid=(batch, RET_HEADS // hp),
        in_specs=in_specs,
        out_specs=[
            pl.BlockSpec((seq, hp * RET_DV), lambda b, h: (b, h)),
            pl.BlockSpec((1, hp, RET_DK, RET_DV), st),
            pl.BlockSpec((1, hp, RET_DK, RET_DV), st),
        ],
        out_shape=[jax.ShapeDtypeStruct((batch * seq, RET_HEADS * RET_DV), BF16), st_shape, st_shape],
        scratch_shapes=[pltpu.VMEM((seq, hp * RET_DV), F32), pltpu.VMEM((seq, hp * RET_DV), F32)],
        compiler_params=pltpu.CompilerParams(
            dimension_semantics=("parallel", "parallel"), vmem_limit_bytes=VMEM_LIMIT),
        name="ret_b%d" % batch,
    )(*args)


def _merge_kernel(np_blocks, ap_ref, as_ref, rp_ref, rs_ref, sga_ref, sgb_ref, xp_ref, xs_ref, mod_ref, g_ref,
                  wa_ref, wb_ref, wo_ref, x1_ref, h2_ref):
    is_prompt = pl.program_id(0) < np_blocks
    attn = jnp.where(is_prompt, ap_ref[...], as_ref[...])
    ret = jnp.where(is_prompt, rp_ref[...], rs_ref[...])
    x = jnp.where(is_prompt, xp_ref[...], xs_ref[...])
    m = sga_ref[...] * _dot(attn, wa_ref[...]) + sgb_ref[...] * _dot(ret, wb_ref[...])
    o = _dot(m.astype(BF16), wo_ref[...])
    x1 = x + mod_ref[0, 2:3, :] * o
    x1_ref[...] = x1
    h2 = _rms(x1, g_ref[...]) * (1.0 + mod_ref[0, 4:5, :]) + mod_ref[0, 3:4, :]
    h2_ref[...] = h2.astype(BF16)


def _merge_call(attn_p, attn_s, ret_p, ret_s, sga, sgb, xp, xs, mod3, norm_ffn, wa, wb, wo, np_blocks, seq_blocks):
    n = xp.shape[0] + xs.shape[0]
    nblk = n // TM
    const = lambda i: (0, 0)
    row = lambda i: (i, 0)
    p_idx = lambda i: (jnp.minimum(i, np_blocks - 1), 0)
    s_idx = lambda i: (jnp.maximum(i - np_blocks, 0), 0)

    def mod_idx(i):
        return (jnp.where(i < np_blocks, 0, 1 + (i - np_blocks) // seq_blocks), 0, 0)

    w = attn_p.shape[1]
    return pl.pallas_call(
        functools.partial(_merge_kernel, np_blocks),
        grid=(nblk,),
        in_specs=[
            pl.BlockSpec((TM, w), p_idx),
            pl.BlockSpec((TM, w), s_idx),
            pl.BlockSpec((TM, w), p_idx),
            pl.BlockSpec((TM, w), s_idx),
            pl.BlockSpec((TM, D_MODEL), row),
            pl.BlockSpec((TM, D_MODEL), row),
            pl.BlockSpec((TM, D_MODEL), p_idx),
            pl.BlockSpec((TM, D_MODEL), s_idx),
            pl.BlockSpec((1, 8, D_MODEL), mod_idx),
            pl.BlockSpec((1, D_MODEL), const),
            pl.BlockSpec(wa.shape, const),
            pl.BlockSpec(wb.shape, const),
            pl.BlockSpec(wo.shape, const),
        ],
        out_specs=[pl.BlockSpec((TM, D_MODEL), row), pl.BlockSpec((TM, D_MODEL), row)],
        out_shape=[jax.ShapeDtypeStruct((n, D_MODEL), F32), jax.ShapeDtypeStruct((n, D_MODEL), BF16)],
        compiler_params=pltpu.CompilerParams(dimension_semantics=("parallel",), vmem_limit_bytes=VMEM_LIMIT),
        name="merge",
    )(attn_p, attn_s, ret_p, ret_s, sga, sgb, xp, xs, mod3, norm_ffn, wa, wb, wo)


def _top_rows(s, k):
    rows = []
    for _ in range(k):
        m = jnp.max(s, axis=0, keepdims=True)
        rows.append(m)
        s = jnp.where(s == m, -jnp.inf, s)
    return rows


def _sort_pairs(n):
    pairs = []
    p = 1
    while p < n:
        k = p
        while k >= 1:
            for j in range(k % p, n - k, 2 * k):
                for i in range(min(k, n - j - k)):
                    if (i + j) // (2 * p) == (i + j + k) // (2 * p):
                        pairs.append((i + j, i + j + k))
            k //= 2
        p *= 2
    return pairs


def _exchange(vs, i, j):
    vs[i], vs[j] = jnp.maximum(vs[i], vs[j]), jnp.minimum(vs[i], vs[j])


def _top_sorted(blocks):
    k = len(blocks)
    vs = list(blocks)
    for i, j in _sort_pairs(k):
        _exchange(vs, i, j)
    shift = SUBLANES // 2
    while shift >= 1:
        other = [pltpu.roll(v, shift, 0) for v in vs]
        vs = [jnp.maximum(vs[i], other[k - 1 - i]) for i in range(k)]
        d = k // 2
        while d >= 1:
            for i in range(k):
                if i & d == 0:
                    _exchange(vs, i, i + d)
            d //= 2
        shift //= 2
    return vs


def _stack_rows(rows):
    n, cols = len(rows), rows[0].shape[1]
    ridx = lax.broadcasted_iota(jnp.int32, (n, cols), 0)
    out = jnp.broadcast_to(rows[0], (n, cols))
    for a in range(1, n):
        out = jnp.where(ridx == a, rows[a], out)
    return out


def _route_tile(s1, s2):
    nb = N_KEYS // SUBLANES
    assert nb == PEER_TOPK
    b1 = [s1[SUBLANES * i:SUBLANES * (i + 1), :] for i in range(nb)]
    b2 = [s2[SUBLANES * i:SUBLANES * (i + 1), :] for i in range(nb)]
    t1 = _top_sorted(b1)
    t2 = _top_sorted(b2)
    r1 = [t[0:1, :] for t in t1]
    r2 = [t[0:1, :] for t in t2]
    v1 = _stack_rows(r1)
    v2 = _stack_rows(r2)
    k8 = PEER_TOPK // 2
    cands = [r1[0] + v2, r1[1] + v2[0:k8, :], r1[2] + v2[0:k8, :], r1[3] + v2[0:k8, :],
             v1 + r2[0], v1[0:k8, :] + r2[1], v1[0:k8, :] + r2[2]]
    tau = _top_rows(jnp.concatenate(cands, axis=0), PEER_TOPK)[-1]
    e1 = [jnp.exp(r - r1[0]) for r in r1]
    e2v = jnp.exp(v2 - r2[0])
    cnts, zs = [], []
    for a in range(PEER_TOPK):
        sel = (r1[a] + v2) >= tau
        cnts.append(jnp.sum(jnp.where(sel, 1.0, 0.0), axis=0, keepdims=True))
        zs.append(jnp.sum(jnp.where(sel, e2v, 0.0), axis=0, keepdims=True))
    z = e1[0] * zs[0]
    for a in range(1, PEER_TOPK):
        z = z + e1[a] * zs[a]
    zinv = 1.0 / z
    cnts = [jnp.broadcast_to(c, b1[0].shape) for c in cnts]
    rank2, cnt = [], []
    for i in range(nb):
        rk = jnp.zeros(b2[i].shape, F32)
        ct = jnp.zeros(b1[i].shape, F32)
        for a in range(PEER_TOPK):
            rk = jnp.where(t2[a] > b2[i], float(a + 1), rk)
            ct = jnp.where(b1[i] == t1[a], cnts[a], ct)
        rank2.append(rk)
        cnt.append(ct)
    rank2 = jnp.concatenate(rank2, axis=0)
    cnt = jnp.concatenate(cnt, axis=0)
    e2 = jnp.exp(s2 - r2[0])
    cc = jnp.exp(s1 - r1[0]) * zinv
    return rank2, e2, cnt, cc


def _peer_route(qt_ref, k1_ref, k2_ref, r2_ref, e2_ref, cnt_ref, cc_ref):
    half = PEER_DQ // 2
    tm = qt_ref.shape[1]

    def head(hd, carry):
        q0 = pl.multiple_of(hd * PEER_DQ, PEER_DQ)
        k0 = pl.multiple_of(hd * N_KEYS, N_KEYS)
        k1 = k1_ref[pl.ds(k0, N_KEYS), :]
        k2 = k2_ref[pl.ds(k0, N_KEYS), :]
        for st in range(tm // LANE):
            lanes = slice(st * LANE, (st + 1) * LANE)
            s1 = _dot(k1, qt_ref[pl.ds(q0, half), lanes].astype(BF16))
            s2 = _dot(k2, qt_ref[pl.ds(q0 + half, half), lanes].astype(BF16))
            rank2, e2, cnt, cc = _route_tile(s1, s2)
            r2_ref[pl.ds(k0, N_KEYS), lanes] = rank2.astype(BF16)
            e2_ref[pl.ds(k0, N_KEYS), lanes] = e2.astype(BF16)
            cnt_ref[pl.ds(k0, N_KEYS), lanes] = cnt
            cc_ref[pl.ds(k0, N_KEYS), lanes] = cc
        return carry

    lax.fori_loop(0, PEER_HEADS, head, 0)


def _peer_kernel(h2_ref, x1_ref, mod_ref, g_ref, wqt_ref, k1_ref, k2_ref, u_ref, vt_ref,
                 y_ref, qt_ref, r2_ref, e2_ref, cnt_ref, cc_ref, acc_ref):
    j = pl.program_id(1)
    nj = pl.num_programs(1)
    tm = h2_ref.shape[0]

    @pl.when(j == 0)
    def _():
        qt_ref[...] = _dot_nt(wqt_ref[...], h2_ref[...])
        _peer_route(qt_ref, k1_ref, k2_ref, r2_ref, e2_ref, cnt_ref, cc_ref)
        acc_ref[...] = jnp.zeros(acc_ref.shape, F32)

    parts = []
    for il in range(E_CHUNK // N_KEYS):
        if il % PEER_KEYS_PER_DOT == 0:
            rows = slice(il * N_KEYS, (il + PEER_KEYS_PER_DOT) * N_KEYS)
            hu = _dot_nt(h2_ref[...], u_ref[rows, :])
            act = jax.nn.gelu(hu.T.astype(BF16), approximate=True)
        g = None
        for hd in range(PEER_HEADS):
            r = hd * N_KEYS + j * (E_CHUNK // N_KEYS) + il
            cnt = jnp.broadcast_to(cnt_ref[pl.ds(r, 1), :], (N_KEYS, tm)).astype(BF16)
            cc = jnp.broadcast_to(cc_ref[pl.ds(r, 1), :], (N_KEYS, tm)).astype(BF16)
            kr = slice(hd * N_KEYS, (hd + 1) * N_KEYS)
            e2 = e2_ref[kr, :]
            t = jnp.where(r2_ref[kr, :] < cnt, e2, jnp.zeros_like(e2)) * cc
            g = t if g is None else g + t
        ia = il % PEER_KEYS_PER_DOT
        parts.append(g * act[ia * N_KEYS:(ia + 1) * N_KEYS, :])
    p = jnp.concatenate(parts, axis=0)
    acc_ref[...] += _dot(vt_ref[...], p)

    @pl.when(j == nj - 1)
    def _():
        x2 = x1_ref[...] + mod_ref[0, 5:6, :] * acc_ref[...].T
        y_ref[...] = _rms(x2, g_ref[...])


def _peer_call(h2, x1, mod3, norm_final, wqt, k1, k2, u_b, vt_b, tok_off, n, seq):
    tm = TM_PEER
    nblk = n // tm
    off = tok_off // tm
    ne = u_b.shape[0] // E_CHUNK
    nq = PEER_HEADS * PEER_DQ
    nk = PEER_HEADS * N_KEYS

    def mod_idx(i, j):
        return (0 if seq is None else 1 + i // (seq // tm), 0, 0)

    const = lambda i, j: (0, 0)
    tok = lambda i, j: (off + i, 0)
    return pl.pallas_call(
        _peer_kernel,
        grid=(nblk, ne),
        in_specs=[
            pl.BlockSpec((tm, D_MODEL), tok),
            pl.BlockSpec((tm, D_MODEL), tok),
            pl.BlockSpec((1, 8, D_MODEL), mod_idx),
            pl.BlockSpec((1, D_MODEL), const),
            pl.BlockSpec(wqt.shape, const),
            pl.BlockSpec(k1.shape, const),
            pl.BlockSpec(k2.shape, const),
            pl.BlockSpec((E_CHUNK, D_MODEL), lambda i, j: (j, 0)),
            pl.BlockSpec((D_MODEL, E_CHUNK), lambda i, j: (0, j)),
        ],
        out_specs=pl.BlockSpec((tm, D_MODEL), lambda i, j: (i, 0)),
        out_shape=jax.ShapeDtypeStruct((n, D_MODEL), F32),
        scratch_shapes=[
            pltpu.VMEM((nq, tm), F32),
            pltpu.VMEM((nk, tm), BF16),
            pltpu.VMEM((nk, tm), BF16),
            pltpu.VMEM((nk, tm), F32),
            pltpu.VMEM((nk, tm), F32),
            pltpu.VMEM((D_MODEL, tm), F32),
        ],
        compiler_params=pltpu.CompilerParams(
            dimension_semantics=("parallel", "arbitrary"), vmem_limit_bytes=VMEM_LIMIT),
        name="peer_n%d" % n,
    )(h2, x1, mod3, norm_final, wqt, k1, k2, u_b, vt_b)


def _rope_tables(seq):
    rows = seq // GRID_W
    row = jnp.repeat(jnp.arange(rows, dtype=F32), GRID_W)
    col = jnp.tile(jnp.arange(GRID_W, dtype=F32), rows)
    nf = MLA_ROPE // 4
    freqs = jnp.power(ROPE_BASE, -jnp.arange(nf, dtype=F32) / nf)
    ang = jnp.concatenate([row[:, None] * freqs, col[:, None] * freqs], axis=-1)
    cos, sin = jnp.cos(ang), jnp.sin(ang)
    pad = jnp.zeros((seq, LANE - MLA_ROPE), F32)
    ctab = jnp.concatenate([cos, cos, pad], axis=1)
    stab = jnp.concatenate([-sin, sin, pad], axis=1)
    cid = jnp.concatenate([jnp.ones((TM, MLA_ROPE), F32), jnp.zeros((TM, LANE - MLA_ROPE), F32)], axis=1)
    ctab = jnp.concatenate([ctab, cid], axis=0)
    stab = jnp.concatenate([stab, jnp.zeros((TM, LANE), F32)], axis=0)
    return ctab, stab


def _pad_cols(a, width):
    return jnp.concatenate([a, jnp.zeros((a.shape[0], width - a.shape[1]), a.dtype)], axis=1)


def kernel(x_prompt, x_sample, c, cache_ckv, cache_krope, state_ret_fwd, state_ret_bwd, c_ctx, w_mod, b_mod, norm_mix, norm_ffn, norm_final, w_in, q_norm, kv_norm, w_uq, w_ukv, ret_logit_fwd, ret_logit_bwd, ret_gn, w_up_a, w_up_b, w_o, peer_wq, peer_keys1, peer_keys2, peer_u, peer_v):
    depth = w_mod.shape[0]
    assert depth == 1
    bp, sp, _ = x_prompt.shape
    bs, ss, _ = x_sample.shape
    past = cache_ckv.shape[2]
    n_p, n_s = bp * sp, bs * ss
    assert 1 + bs <= MOD_ROWS and sp == TM and ss % TM == 0 and ss % TM_PEER == 0 and n_p % TM_PEER == 0
    np_blocks, seq_blocks = n_p // TM, ss // TM
    l = 0

    half = MLA_ROPE // 2
    kr1 = w_in[l][:, 384:384 + half]
    kr2 = w_in[l][:, 384 + half:384 + MLA_ROPE]
    win_p = jnp.concatenate([
        w_in[l][:, :384],
        _pad_cols(jnp.concatenate([kr1, kr2], axis=1), LANE),
        _pad_cols(jnp.concatenate([kr2, kr1], axis=1), LANE),
        w_in[l][:, 384 + MLA_ROPE:],
    ], axis=1).astype(BF16)
    assert win_p.shape[1] == _D_IN_P
    w3 = w_uq[l].reshape(Q_LORA, MLA_HEADS, MLA_NOPE + MLA_ROPE)
    r1 = w3[:, :, MLA_NOPE:MLA_NOPE + half]
    r2 = w3[:, :, MLA_NOPE + half:]
    zpad = jnp.zeros((Q_LORA, MLA_HEADS, LANE - MLA_ROPE), F32)
    wuq_all = jnp.concatenate([
        w3[:, :, :MLA_NOPE].reshape(Q_LORA, MLA_HEADS * MLA_NOPE),
        jnp.concatenate([r1, r2, zpad], axis=2).reshape(Q_LORA, MLA_HEADS * LANE),
        jnp.concatenate([r2, r1, zpad], axis=2).reshape(Q_LORA, MLA_HEADS * LANE),
    ], axis=1).astype(BF16)
    wkv3 = w_ukv[l].reshape(KV_LORA, MLA_HEADS, MLA_NOPE + MLA_V)
    eye = jnp.eye(MLA_HEADS, dtype=F32)
    wk_hdl = jnp.transpose(wkv3[:, :, :MLA_NOPE], (1, 2, 0))
    wv_hld = jnp.transpose(wkv3[:, :, MLA_NOPE:], (1, 0, 2))
    bdk = (wk_hdl[:, :, None, :] * eye[:, None, :, None]).reshape(
        MLA_HEADS * MLA_NOPE, MLA_HEADS * KV_LORA).astype(BF16)
    bdv = (wv_hld[:, :, None, :] * eye[:, None, :, None]).reshape(
        MLA_HEADS * KV_LORA, MLA_HEADS * MLA_V).astype(BF16)
    wa = w_up_a[l].astype(BF16)
    wb = w_up_b[l].astype(BF16)
    wo = w_o[l].astype(BF16)
    wqt = peer_wq[l].T.astype(BF16)
    k1 = peer_keys1[l].reshape(PEER_HEADS * N_KEYS, PEER_DQ // 2).astype(BF16)
    k2 = peer_keys2[l].reshape(PEER_HEADS * N_KEYS, PEER_DQ // 2).astype(BF16)
    u_b = peer_u[l].astype(BF16)
    vt_b = peer_v[l].T.astype(BF16)
    ctab, stab = _rope_tables(ss)
    lf = jnp.broadcast_to(ret_logit_fwd[l][:, None, None], (RET_HEADS, 1, LANE))
    lb = jnp.broadcast_to(ret_logit_bwd[l][:, None, None], (RET_HEADS, 1, LANE))

    cc = jnp.concatenate([c_ctx[None, :], c, jnp.zeros((MOD_ROWS - 1 - bs, D_MODEL), F32)], axis=0)
    mod = _mod_call(cc, w_mod[l], b_mod[l][None, :])
    mod3 = jnp.concatenate([mod.reshape(MOD_ROWS, 6, D_MODEL), jnp.zeros((MOD_ROWS, 2, D_MODEL), F32)], axis=1)

    xp = x_prompt.reshape(n_p, D_MODEL)
    xs = x_sample.reshape(n_s, D_MODEL)
    qt, kk, ckvt, ckv32, kr32, rq, rk, rv, srg, sga, sgb = _proj_call(
        xp, xs, mod3, norm_mix[l][None, :], win_p, q_norm[l][None, :], kv_norm[l][None, :], wuq_all.T, bdk.T,
        ctab, stab, np_blocks, seq_blocks)

    kk_p = kk[:n_p].reshape(bp, sp, 2 * LANE)
    vt_p = jnp.transpose(ckvt[:, :n_p].reshape(KV_LORA, bp, sp), (1, 0, 2))
    cache_kk = jnp.concatenate([cache_ckv[:, l], _pad_cols(
        cache_krope[:, l].reshape(bs * past, MLA_ROPE), LANE).reshape(bs, past, LANE)], axis=2).astype(BF16)
    kk_s = jnp.concatenate([kk[n_p:].reshape(bs, ss, 2 * LANE), cache_kk], axis=1)
    vt_s = jnp.concatenate([jnp.transpose(ckvt[:, n_p:].reshape(KV_LORA, bs, ss), (1, 0, 2)),
                            jnp.transpose(cache_ckv[:, l], (0, 2, 1)).astype(BF16)], axis=2)
    attn_p = _attn_call(qt, kk_p, vt_p, bdv, 0, bp, sp, sp)
    attn_s = _attn_call(qt, kk_s, vt_s, bdv, n_p, bs, ss, 512)

    gn = ret_gn[l][None, :]
    ret_p, sf, sb = _ret_call(rq, rk, rv, srg, gn, lf, lb, None, None, 0, bp, sp)
    ret_s, _, _ = _ret_call(rq, rk, rv, srg, gn, lf, lb, state_ret_fwd[:, l], state_ret_bwd[:, l], n_p, bs, ss)

    x1, h2 = _merge_call(attn_p, attn_s, ret_p, ret_s, sga, sgb, xp, xs, mod3, norm_ffn[l][None, :],
                          wa, wb, wo, np_blocks, seq_blocks)

    nf = norm_final[None, :]
    y_prompt = _peer_call(h2, x1, mod3, nf, wqt, k1, k2, u_b, vt_b, 0, n_p, None).reshape(bp, sp, D_MODEL)
    y_sample = _peer_call(h2, x1, mod3, nf, wqt, k1, k2, u_b, vt_b, n_p, n_s, ss).reshape(bs, ss, D_MODEL)
    new_ckv = ckv32[:n_p].reshape(bp, 1, sp, KV_LORA)
    new_kr = kr32[:n_p, :MLA_ROPE].reshape(bp, 1, sp, MLA_ROPE)
    return (y_prompt, y_sample, new_ckv, new_kr, sf[:, None], sb[:, None])
```

```python
import functools

import jax
import jax.numpy as jnp
from jax import lax
from jax.experimental import pallas as pl
from jax.experimental.pallas import tpu as pltpu

F32 = jnp.float32
BF16 = jnp.bfloat16

D_MODEL = 1024
GRID_W = 64
EPS = 1e-6
MLA_HEADS = 8
MLA_NOPE = 64
MLA_ROPE = 32
MLA_V = 64
Q_LORA = 256
KV_LORA = 128
ROPE_BASE = 10000.0
RET_HEADS = 4
RET_DK = 128
RET_DV = 128
PEER_HEADS = 8
PEER_DQ = 256
N_KEYS = 128
PEER_TOPK = 16

LANE = 128
SUBLANES = 8
LOG2_E = 1.4426950408889634
VMEM_LIMIT = 56 * 1024 * 1024

TM = 256
TQ = 128
RET_CHUNK = 256
RET_HEADS_PER_STEP = 4
TM_PEER = 512
E_CHUNK = 2048
PEER_KEYS_PER_DOT = 2
MOD_ROWS = 16

_C_CQ = 0
_C_CKV = 256
_C_KR = 384
_C_KRS = 512
_C_RQ = 640
_C_RK = 1152
_C_RV = 1664
_C_RG = 2176
_C_GA = 2688
_C_GB = 3712
_D_IN_P = 4736


def _rms(x, g):
    return x * lax.rsqrt(jnp.mean(x * x, axis=-1, keepdims=True) + EPS) * g


def _dot(a, b):
    return jnp.dot(a, b, preferred_element_type=F32)


def _dot_nt(a, b):
    return lax.dot_general(a, b, (((1,), (1,)), ((), ())), preferred_element_type=F32)


def _mod_kernel(c_ref, w_ref, b_ref, o_ref):
    c = c_ref[...]
    s = c * jax.nn.sigmoid(c)
    o_ref[...] = _dot(s.astype(BF16), w_ref[...].astype(BF16)) + b_ref[...]


def _mod_call(cc, w_mod, b_mod):
    n = w_mod.shape[1]
    bn = 1024
    return pl.pallas_call(
        _mod_kernel,
        grid=(n // bn,),
        in_specs=[
            pl.BlockSpec((MOD_ROWS, D_MODEL), lambda j: (0, 0)),
            pl.BlockSpec((D_MODEL, bn), lambda j: (0, j)),
            pl.BlockSpec((1, bn), lambda j: (0, j)),
        ],
        out_specs=pl.BlockSpec((MOD_ROWS, bn), lambda j: (0, j)),
        out_shape=jax.ShapeDtypeStruct((MOD_ROWS, n), F32),
        compiler_params=pltpu.CompilerParams(dimension_semantics=("parallel",), vmem_limit_bytes=VMEM_LIMIT),
        name="mod",
    )(cc, w_mod, b_mod)


def _proj_kernel(np_blocks, xp_ref, xs_ref, mod_ref, g_ref, win_ref, qn_ref, kvn_ref, wuqt_ref, bdkt_ref,
                 ct_ref, st_ref, ctt_ref, stt_ref,
                 qt_ref, kk_ref, ckvt_ref, ckv_ref, kr_ref, rq_ref, rk_ref, rv_ref, srg_ref, sga_ref, sgb_ref):
    x = jnp.where(pl.program_id(0) < np_blocks, xp_ref[...], xs_ref[...])
    h = _rms(x, g_ref[...]) * (1.0 + mod_ref[0, 1:2, :]) + mod_ref[0, 0:1, :]
    y = _dot(h.astype(BF16), win_ref[...])

    ct = ct_ref[...]
    st = st_ref[...]
    ckv = _rms(y[:, _C_CKV:_C_CKV + KV_LORA], kvn_ref[...])
    kr = y[:, _C_KR:_C_KR + LANE]
    krot = kr * ct + y[:, _C_KRS:_C_KRS + LANE] * st
    ckv_ref[...] = ckv
    kr_ref[...] = kr
    kk_ref[:, 0:KV_LORA] = ckv.astype(BF16)
    kk_ref[:, KV_LORA:2 * KV_LORA] = krot.astype(BF16)
    ckvt_ref[...] = ckv.T.astype(BF16)

    cq = _rms(y[:, _C_CQ:_C_CQ + Q_LORA], qn_ref[...])
    qat = _dot(wuqt_ref[...], cq.T.astype(BF16))
    nn = MLA_HEADS * MLA_NOPE
    nr = MLA_HEADS * LANE
    qlat = _dot(bdkt_ref[...], qat[:nn, :].astype(BF16))
    ctt = ctt_ref[...]
    stt = stt_ref[...]
    scale = (MLA_NOPE + MLA_ROPE) ** -0.5 * LOG2_E
    for hd in range(MLA_HEADS):
        lo = hd * LANE
        qrot = qat[nn + lo:nn + lo + LANE, :] * ctt + qat[nn + nr + lo:nn + nr + lo + LANE, :] * stt
        qt_ref[2 * lo:2 * lo + LANE, :] = (qlat[lo:lo + LANE, :] * scale).astype(BF16)
        qt_ref[2 * lo + LANE:2 * lo + 2 * LANE, :] = (qrot * scale).astype(BF16)

    hk = RET_HEADS * RET_DK
    rq_ref[...] = y[:, _C_RQ:_C_RQ + hk].astype(BF16)
    rk_ref[...] = (y[:, _C_RK:_C_RK + hk] * (RET_DK ** -0.5)).astype(BF16)
    rv_ref[...] = y[:, _C_RV:_C_RV + hk].astype(BF16)
    rg = y[:, _C_RG:_C_RG + hk]
    srg_ref[...] = (rg * jax.nn.sigmoid(rg)).astype(BF16)
    sga_ref[...] = jax.nn.sigmoid(y[:, _C_GA:_C_GA + D_MODEL]).astype(BF16)
    sgb_ref[...] = jax.nn.sigmoid(y[:, _C_GB:_C_GB + D_MODEL]).astype(BF16)


def _proj_call(xp, xs, mod3, norm_mix, win_p, q_norm, kv_norm, wuq_t, bdk_t, ctab, stab, np_blocks, seq_blocks):
    n = xp.shape[0] + xs.shape[0]
    nblk = n // TM
    rope_id_block = ctab.shape[0] // TM - 1

    def mod_idx(i):
        return (jnp.where(i < np_blocks, 0, 1 + (i - np_blocks) // seq_blocks), 0, 0)

    def rope_blk(i):
        return jnp.where(i < np_blocks, rope_id_block, (i - np_blocks) % seq_blocks)

    rope_idx = lambda i: (rope_blk(i), 0)
    rope_idx_t = lambda i: (0, rope_blk(i))
    const = lambda i: (0, 0)
    row = lambda i: (i, 0)
    col = lambda i: (0, i)
    hk = RET_HEADS * RET_DK
    out_shape = [
        jax.ShapeDtypeStruct((2 * MLA_HEADS * LANE, n), BF16),
        jax.ShapeDtypeStruct((n, 2 * LANE), BF16),
        jax.ShapeDtypeStruct((KV_LORA, n), BF16),
        jax.ShapeDtypeStruct((n, KV_LORA), F32),
        jax.ShapeDtypeStruct((n, LANE), F32),
        jax.ShapeDtypeStruct((n, hk), BF16),
        jax.ShapeDtypeStruct((n, hk), BF16),
        jax.ShapeDtypeStruct((n, hk), BF16),
        jax.ShapeDtypeStruct((n, hk), BF16),
        jax.ShapeDtypeStruct((n, D_MODEL), BF16),
        jax.ShapeDtypeStruct((n, D_MODEL), BF16),
    ]
    out_specs = [pl.BlockSpec((s.shape[0], TM), col) if s.shape[1] == n else pl.BlockSpec((TM, s.shape[1]), row)
                 for s in out_shape]
    return pl.pallas_call(
        functools.partial(_proj_kernel, np_blocks),
        grid=(nblk,),
        in_specs=[
            pl.BlockSpec((TM, D_MODEL), lambda i: (jnp.minimum(i, np_blocks - 1), 0)),
            pl.BlockSpec((TM, D_MODEL), lambda i: (jnp.maximum(i - np_blocks, 0), 0)),
            pl.BlockSpec((1, 8, D_MODEL), mod_idx),
            pl.BlockSpec((1, D_MODEL), const),
            pl.BlockSpec(win_p.shape, const),
            pl.BlockSpec((1, Q_LORA), const),
            pl.BlockSpec((1, KV_LORA), const),
            pl.BlockSpec(wuq_t.shape, const),
            pl.BlockSpec(bdk_t.shape, const),
            pl.BlockSpec((TM, LANE), rope_idx),
            pl.BlockSpec((TM, LANE), rope_idx),
            pl.BlockSpec((LANE, TM), rope_idx_t),
            pl.BlockSpec((LANE, TM), rope_idx_t),
        ],
        out_specs=out_specs,
        out_shape=out_shape,
        compiler_params=pltpu.CompilerParams(dimension_semantics=("parallel",), vmem_limit_bytes=VMEM_LIMIT),
        name="proj",
    )(xp, xs, mod3, norm_mix, win_p, q_norm, kv_norm, wuq_t, bdk_t, ctab, stab, ctab.T, stab.T)


def _attn_kernel(seq, past, tkb, *refs):
    if past:
        qt_ref, k_ref, vt_ref, ck_ref, cvt_ref, bdv_ref, o_ref, s_ref = refs
    else:
        qt_ref, k_ref, vt_ref, bdv_ref, o_ref, s_ref = refs
    blocks = []
    for kb in range(seq // tkb):
        rows = slice(kb * tkb, (kb + 1) * tkb)
        blocks.append((rows, k_ref[rows, :], vt_ref[:, rows]))
    if past:
        blocks.append((slice(seq, seq + past), ck_ref[0], cvt_ref[0]))
    fw = 2 * LANE
    qt = jnp.concatenate([qt_ref[hd * fw:(hd + 1) * fw, :] for hd in range(MLA_HEADS)], axis=1)
    m = None
    for rows, keys, _ in blocks:
        s = _dot(keys, qt)
        s_ref[rows, :] = s
        mk = jnp.max(s, axis=0, keepdims=True)
        m = mk if m is None else jnp.maximum(m, mk)
    l = None
    acc = None
    for rows, _, lat_t in blocks:
        p = jnp.exp2(s_ref[rows, :] - m)
        lk = jnp.sum(p, axis=0, keepdims=True)
        ak = _dot(lat_t, p.astype(BF16))
        l = lk if l is None else l + lk
        acc = ak if acc is None else acc + ak
    o = acc / l
    ocat = jnp.concatenate([o[:, hd * TQ:(hd + 1) * TQ].T for hd in range(MLA_HEADS)], axis=1)
    o_ref[...] = _dot(ocat.astype(BF16), bdv_ref[...]).astype(BF16)


def _attn_call(qt, kk, ckvt, cache_kk, cache_vt, bdv, tok_off, batch, seq, tkb):
    nq = seq // TQ
    past = 0 if cache_kk is None else cache_kk.shape[1]
    off = tok_off // TQ
    off_seq = tok_off // seq
    in_specs = [
        pl.BlockSpec((qt.shape[0], TQ), lambda b, qi: (0, off + b * nq + qi)),
        pl.BlockSpec((seq, 2 * LANE), lambda b, qi: (off_seq + b, 0)),
        pl.BlockSpec((KV_LORA, seq), lambda b, qi: (0, off_seq + b)),
    ]
    args = [qt, kk, ckvt]
    if past:
        in_specs += [pl.BlockSpec((1, past, 2 * LANE), lambda b, qi: (b, 0, 0)),
                     pl.BlockSpec((1, KV_LORA, past), lambda b, qi: (b, 0, 0))]
        args += [cache_kk, cache_vt]
    in_specs.append(pl.BlockSpec(bdv.shape, lambda b, qi: (0, 0)))
    args.append(bdv)
    return pl.pallas_call(
        functools.partial(_attn_kernel, seq, past, tkb),
        grid=(batch, nq),
        in_specs=in_specs,
        out_specs=pl.BlockSpec((TQ, MLA_HEADS * MLA_V), lambda b, qi: (b * nq + qi, 0)),
        out_shape=jax.ShapeDtypeStruct((batch * seq, MLA_HEADS * MLA_V), BF16),
        scratch_shapes=[pltpu.VMEM((seq + past, MLA_HEADS * TQ), F32)],
        compiler_params=pltpu.CompilerParams(
            dimension_semantics=("parallel", "parallel"), vmem_limit_bytes=VMEM_LIMIT),
        name="attn_b%d" % batch,
    )(*args)


def _log_sigmoid(x):
    return jnp.minimum(x, 0.0) - jnp.log(1.0 + jnp.exp(-jnp.abs(x)))


def _ret_kernel(has_state, seq, *refs):
    if has_state:
        (q_ref, k_ref, v_ref, srg_ref, gn_ref, lf_ref, lb_ref, s0f_ref, s0b_ref,
         o_ref, sf_ref, sb_ref, yf_ref, yb_ref) = refs
    else:
        (q_ref, k_ref, v_ref, srg_ref, gn_ref, lf_ref, lb_ref,
         o_ref, sf_ref, sb_ref, yf_ref, yb_ref) = refs
    c = RET_CHUNK
    n = seq // c
    ii = lax.broadcasted_iota(jnp.int32, (c, c), 0)
    jj = lax.broadcasted_iota(jnp.int32, (c, c), 1)
    diff = (ii - jj).astype(F32)
    idx = lax.broadcasted_iota(jnp.int32, (c, 1), 0).astype(F32)

    consts = []
    for hh in range(RET_HEADS_PER_STEP):
        lgf = _log_sigmoid(lf_ref[hh, :, 0:1])
        lgb = _log_sigmoid(lb_ref[hh, :, 0:1])
        fwd = (jnp.where(diff >= 0, jnp.exp(lgf * jnp.maximum(diff, 0.0)), 0.0),
               jnp.exp(lgf * (idx + 1.0)), jnp.exp(lgf * (c - 1.0 - idx)), jnp.exp(lgf * c))
        bwd = (jnp.where(diff <= 0, jnp.exp(lgb * jnp.maximum(-diff, 0.0)), 0.0),
               jnp.exp(lgb * (c - idx)), jnp.exp(lgb * idx), jnp.exp(lgb * c))
        consts.append((fwd, bwd))

    def chunk(hh, start, state, dec, y_ref):
        dmat, qdec, kdec, cdec = dec
        lanes = slice(hh * RET_DK, (hh + 1) * RET_DK)
        q = q_ref[pl.ds(start, c), lanes]
        k = k_ref[pl.ds(start, c), lanes]
        v = v_ref[pl.ds(start, c), lanes]
        sc = _dot_nt(q, k) * dmat
        y = _dot(sc.astype(BF16), v) + _dot(q, state.astype(BF16)) * qdec
        y_ref[pl.ds(start, c), lanes] = y
        kdt = (k.astype(F32) * kdec).T.astype(BF16)
        return state * cdec + _dot(kdt, v)

    def body(t, carry):
        out = []
        for hh in range(RET_HEADS_PER_STEP):
            sf, sb = carry[hh]
            sf = chunk(hh, pl.multiple_of(t * c, c), sf, consts[hh][0], yf_ref)
            sb = chunk(hh, pl.multiple_of((n - 1 - t) * c, c), sb, consts[hh][1], yb_ref)
            out.append((sf, sb))
        return tuple(out)

    if has_state:
        init = tuple((s0f_ref[0, hh], s0b_ref[0, hh]) for hh in range(RET_HEADS_PER_STEP))
    else:
        zero = jnp.zeros((RET_DK, RET_DV), F32)
        init = tuple((zero, zero) for _ in range(RET_HEADS_PER_STEP))
    final = lax.fori_loop(0, n, body, init)
    for hh in range(RET_HEADS_PER_STEP):
        sf_ref[0, hh] = final[hh][0]
        sb_ref[0, hh] = final[hh][1]

    for hh in range(RET_HEADS_PER_STEP):
        lanes = slice(hh * RET_DV, (hh + 1) * RET_DV)
        y = yf_ref[:, lanes] + yb_ref[:, lanes]
        mu = jnp.mean(y, axis=-1, keepdims=True)
        yc = y - mu
        var = jnp.mean(yc * yc, axis=-1, keepdims=True)
        yn = yc * lax.rsqrt(var + EPS)
        o_ref[:, lanes] = (srg_ref[:, lanes] * (yn * gn_ref[:, lanes])).astype(BF16)


def _ret_call(rq, rk, rv, srg, gn, lf, lb, s0f, s0b, tok_off, batch, seq):
    off = tok_off // seq
    has_state = s0f is not None
    hp = RET_HEADS_PER_STEP
    tokblk = lambda b, h: (off + b, h)
    st = lambda b, h: (b, h, 0, 0)
    in_specs = [
        pl.BlockSpec((seq, hp * RET_DK), tokblk),
        pl.BlockSpec((seq, hp * RET_DK), tokblk),
        pl.BlockSpec((seq, hp * RET_DV), tokblk),
        pl.BlockSpec((seq, hp * RET_DV), tokblk),
        pl.BlockSpec((1, hp * RET_DV), lambda b, h: (0, h)),
        pl.BlockSpec((hp, 1, LANE), lambda b, h: (h, 0, 0)),
        pl.BlockSpec((hp, 1, LANE), lambda b, h: (h, 0, 0)),
    ]
    args = [rq, rk, rv, srg, gn, lf, lb]
    if has_state:
        in_specs += [pl.BlockSpec((1, hp, RET_DK, RET_DV), st)] * 2
        args += [s0f, s0b]
    st_shape = jax.ShapeDtypeStruct((batch, RET_HEADS, RET_DK, RET_DV), F32)
    return pl.pallas_call(
        functools.partial(_ret_kernel, has_state, seq),
        grid=(batch, RET_HEADS // hp),
        in_specs=in_specs,
        out_specs=[
            pl.BlockSpec((seq, hp * RET_DV), lambda b, h: (b, h)),
            pl.BlockSpec((1, hp, RET_DK, RET_DV), st),
            pl.BlockSpec((1, hp, RET_DK, RET_DV), st),
        ],
        out_shape=[jax.ShapeDtypeStruct((batch * seq, RET_HEADS * RET_DV), BF16), st_shape, st_shape],
        scratch_shapes=[pltpu.VMEM((seq, hp * RET_DV), F32), pltpu.VMEM((seq, hp * RET_DV), F32)],
        compiler_params=pltpu.CompilerParams(
            dimension_semantics=("parallel", "parallel"), vmem_limit_bytes=VMEM_LIMIT),
        name="ret_b%d" % batch,
    )(*args)


def _merge_kernel(np_blocks, ap_ref, as_ref, rp_ref, rs_ref, sga_ref, sgb_ref, xp_ref, xs_ref, mod_ref, g_ref,
                  wa_ref, wb_ref, wo_ref, x1_ref, h2_ref):
    is_prompt = pl.program_id(0) < np_blocks
    attn = jnp.where(is_prompt, ap_ref[...], as_ref[...])
    ret = jnp.where(is_prompt, rp_ref[...], rs_ref[...])
    x = jnp.where(is_prompt, xp_ref[...], xs_ref[...])
    m = sga_ref[...] * _dot(attn, wa_ref[...]) + sgb_ref[...] * _dot(ret, wb_ref[...])
    o = _dot(m.astype(BF16), wo_ref[...])
    x1 = x + mod_ref[0, 2:3, :] * o
    x1_ref[...] = x1
    h2 = _rms(x1, g_ref[...]) * (1.0 + mod_ref[0, 4:5, :]) + mod_ref[0, 3:4, :]
    h2_ref[...] = h2.astype(BF16)


def _merge_call(attn_p, attn_s, ret_p, ret_s, sga, sgb, xp, xs, mod3, norm_ffn, wa, wb, wo, np_blocks, seq_blocks):
    n = xp.shape[0] + xs.shape[0]
    nblk = n // TM
    const = lambda i: (0, 0)
    row = lambda i: (i, 0)
    p_idx = lambda i: (jnp.minimum(i, np_blocks - 1), 0)
    s_idx = lambda i: (jnp.maximum(i - np_blocks, 0), 0)

    def mod_idx(i):
        return (jnp.where(i < np_blocks, 0, 1 + (i - np_blocks) // seq_blocks), 0, 0)

    w = attn_p.shape[1]
    return pl.pallas_call(
        functools.partial(_merge_kernel, np_blocks),
        grid=(nblk,),
        in_specs=[
            pl.BlockSpec((TM, w), p_idx),
            pl.BlockSpec((TM, w), s_idx),
            pl.BlockSpec((TM, w), p_idx),
            pl.BlockSpec((TM, w), s_idx),
            pl.BlockSpec((TM, D_MODEL), row),
            pl.BlockSpec((TM, D_MODEL), row),
            pl.BlockSpec((TM, D_MODEL), p_idx),
            pl.BlockSpec((TM, D_MODEL), s_idx),
            pl.BlockSpec((1, 8, D_MODEL), mod_idx),
            pl.BlockSpec((1, D_MODEL), const),
            pl.BlockSpec(wa.shape, const),
            pl.BlockSpec(wb.shape, const),
            pl.BlockSpec(wo.shape, const),
        ],
        out_specs=[pl.BlockSpec((TM, D_MODEL), row), pl.BlockSpec((TM, D_MODEL), row)],
        out_shape=[jax.ShapeDtypeStruct((n, D_MODEL), F32), jax.ShapeDtypeStruct((n, D_MODEL), BF16)],
        compiler_params=pltpu.CompilerParams(dimension_semantics=("parallel",), vmem_limit_bytes=VMEM_LIMIT),
        name="merge",
    )(attn_p, attn_s, ret_p, ret_s, sga, sgb, xp, xs, mod3, norm_ffn, wa, wb, wo)


def _top_rows(s, k):
    rows = []
    for _ in range(k):
        m = jnp.max(s, axis=0, keepdims=True)
        rows.append(m)
        s = jnp.where(s == m, -jnp.inf, s)
    return rows


def _sort_pairs(n):
    pairs = []
    p = 1
    while p < n:
        k = p
        while k >= 1:
            for j in range(k % p, n - k, 2 * k):
                for i in range(min(k, n - j - k)):
                    if (i + j) // (2 * p) == (i + j + k) // (2 * p):
                        pairs.append((i + j, i + j + k))
            k //= 2
        p *= 2
    return pairs


def _exchange(vs, i, j):
    vs[i], vs[j] = jnp.maximum(vs[i], vs[j]), jnp.minimum(vs[i], vs[j])


def _top_sorted(blocks):
    k = len(blocks)
    vs = list(blocks)
    for i, j in _sort_pairs(k):
        _exchange(vs, i, j)
    shift = SUBLANES // 2
    while shift >= 1:
        other = [pltpu.roll(v, shift, 0) for v in vs]
        vs = [jnp.maximum(vs[i], other[k - 1 - i]) for i in range(k)]
        d = k // 2
        while d >= 1:
            for i in range(k):
                if i & d == 0:
                    _exchange(vs, i, i + d)
            d //= 2
        shift //= 2
    return vs


def _stack_rows(rows):
    n, cols = len(rows), rows[0].shape[1]
    ridx = lax.broadcasted_iota(jnp.int32, (n, cols), 0)
    out = jnp.broadcast_to(rows[0], (n, cols))
    for a in range(1, n):
        out = jnp.where(ridx == a, rows[a], out)
    return out


def _route_tile(s1, s2):
    nb = N_KEYS // SUBLANES
    assert nb == PEER_TOPK
    b1 = [s1[SUBLANES * i:SUBLANES * (i + 1), :] for i in range(nb)]
    b2 = [s2[SUBLANES * i:SUBLANES * (i + 1), :] for i in range(nb)]
    t1 = _top_sorted(b1)
    t2 = _top_sorted(b2)
    r1 = [t[0:1, :] for t in t1]
    r2 = [t[0:1, :] for t in t2]
    v1 = _stack_rows(r1)
    v2 = _stack_rows(r2)
    k8 = PEER_TOPK // 2
    cands = [r1[0] + v2, r1[1] + v2[0:k8, :], r1[2] + v2[0:k8, :], r1[3] + v2[0:k8, :],
             v1 + r2[0], v1[0:k8, :] + r2[1], v1[0:k8, :] + r2[2]]
    tau = _top_rows(jnp.concatenate(cands, axis=0), PEER_TOPK)[-1]
    e1 = [jnp.exp(r - r1[0]) for r in r1]
    e2v = jnp.exp(v2 - r2[0])
    cnts, zs = [], []
    for a in range(PEER_TOPK):
        sel = (r1[a] + v2) >= tau
        cnts.append(jnp.sum(jnp.where(sel, 1.0, 0.0), axis=0, keepdims=True))
        zs.append(jnp.sum(jnp.where(sel, e2v, 0.0), axis=0, keepdims=True))
    z = e1[0] * zs[0]
    for a in range(1, PEER_TOPK):
        z = z + e1[a] * zs[a]
    zinv = 1.0 / z
    cnts = [jnp.broadcast_to(c, b1[0].shape) for c in cnts]
    rank2, cnt = [], []
    for i in range(nb):
        rk = jnp.zeros(b2[i].shape, F32)
        ct = jnp.zeros(b1[i].shape, F32)
        for a in range(PEER_TOPK):
            rk = jnp.where(t2[a] > b2[i], float(a + 1), rk)
            ct = jnp.where(b1[i] == t1[a], cnts[a], ct)
        rank2.append(rk)
        cnt.append(ct)
    rank2 = jnp.concatenate(rank2, axis=0)
    cnt = jnp.concatenate(cnt, axis=0)
    e2 = jnp.exp(s2 - r2[0])
    cc = jnp.exp(s1 - r1[0]) * zinv
    return rank2, e2, cnt, cc


def _peer_route(qt_ref, k1_ref, k2_ref, r2_ref, e2_ref, cnt_ref, cc_ref):
    half = PEER_DQ // 2
    tm = qt_ref.shape[1]

    def head(hd, carry):
        q0 = pl.multiple_of(hd * PEER_DQ, PEER_DQ)
        k0 = pl.multiple_of(hd * N_KEYS, N_KEYS)
        k1 = k1_ref[pl.ds(k0, N_KEYS), :]
        k2 = k2_ref[pl.ds(k0, N_KEYS), :]
        for st in range(tm // LANE):
            lanes = slice(st * LANE, (st + 1) * LANE)
            s1 = _dot(k1, qt_ref[pl.ds(q0, half), lanes].astype(BF16))
            s2 = _dot(k2, qt_ref[pl.ds(q0 + half, half), lanes].astype(BF16))
            rank2, e2, cnt, cc = _route_tile(s1, s2)
            r2_ref[pl.ds(k0, N_KEYS), lanes] = rank2.astype(BF16)
            e2_ref[pl.ds(k0, N_KEYS), lanes] = e2.astype(BF16)
            cnt_ref[pl.ds(k0, N_KEYS), lanes] = cnt
            cc_ref[pl.ds(k0, N_KEYS), lanes] = cc
        return carry

    lax.fori_loop(0, PEER_HEADS, head, 0)


def _peer_kernel(h2_ref, x1_ref, mod_ref, g_ref, wqt_ref, k1_ref, k2_ref, u_ref, vt_ref,
                 y_ref, qt_ref, r2_ref, e2_ref, cnt_ref, cc_ref, acc_ref):
    j = pl.program_id(1)
    nj = pl.num_programs(1)
    tm = h2_ref.shape[0]

    @pl.when(j == 0)
    def _():
        qt_ref[...] = _dot_nt(wqt_ref[...], h2_ref[...])
        _peer_route(qt_ref, k1_ref, k2_ref, r2_ref, e2_ref, cnt_ref, cc_ref)
        acc_ref[...] = jnp.zeros(acc_ref.shape, F32)

    parts = []
    for il in range(E_CHUNK // N_KEYS):
        if il % PEER_KEYS_PER_DOT == 0:
            rows = slice(il * N_KEYS, (il + PEER_KEYS_PER_DOT) * N_KEYS)
            hu = _dot_nt(h2_ref[...], u_ref[rows, :])
            act = jax.nn.gelu(hu.T.astype(BF16), approximate=True)
        g = None
        for hd in range(PEER_HEADS):
            r = hd * N_KEYS + j * (E_CHUNK // N_KEYS) + il
            cnt = jnp.broadcast_to(cnt_ref[pl.ds(r, 1), :], (N_KEYS, tm)).astype(BF16)
            cc = jnp.broadcast_to(cc_ref[pl.ds(r, 1), :], (N_KEYS, tm)).astype(BF16)
            kr = slice(hd * N_KEYS, (hd + 1) * N_KEYS)
            e2 = e2_ref[kr, :]
            t = jnp.where(r2_ref[kr, :] < cnt, e2, jnp.zeros_like(e2)) * cc
            g = t if g is None else g + t
        ia = il % PEER_KEYS_PER_DOT
        parts.append(g * act[ia * N_KEYS:(ia + 1) * N_KEYS, :])
    p = jnp.concatenate(parts, axis=0)
    acc_ref[...] += _dot(vt_ref[...], p)

    @pl.when(j == nj - 1)
    def _():
        x2 = x1_ref[...] + mod_ref[0, 5:6, :] * acc_ref[...].T
        y_ref[...] = _rms(x2, g_ref[...])


def _peer_call(h2, x1, mod3, norm_final, wqt, k1, k2, u_b, vt_b, tok_off, n, seq):
    tm = TM_PEER
    nblk = n // tm
    off = tok_off // tm
    ne = u_b.shape[0] // E_CHUNK
    nq = PEER_HEADS * PEER_DQ
    nk = PEER_HEADS * N_KEYS

    def mod_idx(i, j):
        return (0 if seq is None else 1 + i // (seq // tm), 0, 0)

    const = lambda i, j: (0, 0)
    tok = lambda i, j: (off + i, 0)
    return pl.pallas_call(
        _peer_kernel,
        grid=(nblk, ne),
        in_specs=[
            pl.BlockSpec((tm, D_MODEL), tok),
            pl.BlockSpec((tm, D_MODEL), tok),
            pl.BlockSpec((1, 8, D_MODEL), mod_idx),
            pl.BlockSpec((1, D_MODEL), const),
            pl.BlockSpec(wqt.shape, const),
            pl.BlockSpec(k1.shape, const),
            pl.BlockSpec(k2.shape, const),
            pl.BlockSpec((E_CHUNK, D_MODEL), lambda i, j: (j, 0)),
            pl.BlockSpec((D_MODEL, E_CHUNK), lambda i, j: (0, j)),
        ],
        out_specs=pl.BlockSpec((tm, D_MODEL), lambda i, j: (i, 0)),
        out_shape=jax.ShapeDtypeStruct((n, D_MODEL), F32),
        scratch_shapes=[
            pltpu.VMEM((nq, tm), F32),
            pltpu.VMEM((nk, tm), BF16),
            pltpu.VMEM((nk, tm), BF16),
            pltpu.VMEM((nk, tm), F32),
            pltpu.VMEM((nk, tm), F32),
            pltpu.VMEM((D_MODEL, tm), F32),
        ],
        compiler_params=pltpu.CompilerParams(
            dimension_semantics=("parallel", "arbitrary"), vmem_limit_bytes=VMEM_LIMIT),
        name="peer_n%d" % n,
    )(h2, x1, mod3, norm_final, wqt, k1, k2, u_b, vt_b)


def _rope_tables(seq):
    rows = seq // GRID_W
    row = jnp.repeat(jnp.arange(rows, dtype=F32), GRID_W)
    col = jnp.tile(jnp.arange(GRID_W, dtype=F32), rows)
    nf = MLA_ROPE // 4
    freqs = jnp.power(ROPE_BASE, -jnp.arange(nf, dtype=F32) / nf)
    ang = jnp.concatenate([row[:, None] * freqs, col[:, None] * freqs], axis=-1)
    cos, sin = jnp.cos(ang), jnp.sin(ang)
    pad = jnp.zeros((seq, LANE - MLA_ROPE), F32)
    ctab = jnp.concatenate([cos, cos, pad], axis=1)
    stab = jnp.concatenate([-sin, sin, pad], axis=1)
    cid = jnp.concatenate([jnp.ones((TM, MLA_ROPE), F32), jnp.zeros((TM, LANE - MLA_ROPE), F32)], axis=1)
    ctab = jnp.concatenate([ctab, cid], axis=0)
    stab = jnp.concatenate([stab, jnp.zeros((TM, LANE), F32)], axis=0)
    return ctab, stab


def _pad_cols(a, width):
    return jnp.concatenate([a, jnp.zeros((a.shape[0], width - a.shape[1]), a.dtype)], axis=1)


def kernel(x_prompt, x_sample, c, cache_ckv, cache_krope, state_ret_fwd, state_ret_bwd, c_ctx, w_mod, b_mod, norm_mix, norm_ffn, norm_final, w_in, q_norm, kv_norm, w_uq, w_ukv, ret_logit_fwd, ret_logit_bwd, ret_gn, w_up_a, w_up_b, w_o, peer_wq, peer_keys1, peer_keys2, peer_u, peer_v):
    depth = w_mod.shape[0]
    assert depth == 1
    bp, sp, _ = x_prompt.shape
    bs, ss, _ = x_sample.shape
    past = cache_ckv.shape[2]
    n_p, n_s = bp * sp, bs * ss
    assert 1 + bs <= MOD_ROWS and sp == TM and ss % TM == 0 and ss % TM_PEER == 0 and n_p % TM_PEER == 0
    np_blocks, seq_blocks = n_p // TM, ss // TM
    l = 0

    half = MLA_ROPE // 2
    kr1 = w_in[l][:, 384:384 + half]
    kr2 = w_in[l][:, 384 + half:384 + MLA_ROPE]
    win_p = jnp.concatenate([
        w_in[l][:, :384],
        _pad_cols(jnp.concatenate([kr1, kr2], axis=1), LANE),
        _pad_cols(jnp.concatenate([kr2, kr1], axis=1), LANE),
        w_in[l][:, 384 + MLA_ROPE:],
    ], axis=1).astype(BF16)
    assert win_p.shape[1] == _D_IN_P
    w3 = w_uq[l].reshape(Q_LORA, MLA_HEADS, MLA_NOPE + MLA_ROPE)
    r1 = w3[:, :, MLA_NOPE:MLA_NOPE + half]
    r2 = w3[:, :, MLA_NOPE + half:]
    zpad = jnp.zeros((Q_LORA, MLA_HEADS, LANE - MLA_ROPE), F32)
    wuq_all = jnp.concatenate([
        w3[:, :, :MLA_NOPE].reshape(Q_LORA, MLA_HEADS * MLA_NOPE),
        jnp.concatenate([r1, r2, zpad], axis=2).reshape(Q_LORA, MLA_HEADS * LANE),
        jnp.concatenate([r2, r1, zpad], axis=2).reshape(Q_LORA, MLA_HEADS * LANE),
    ], axis=1).astype(BF16)
    wkv3 = w_ukv[l].reshape(KV_LORA, MLA_HEADS, MLA_NOPE + MLA_V)
    eye = jnp.eye(MLA_HEADS, dtype=F32)
    wk_hdl = jnp.transpose(wkv3[:, :, :MLA_NOPE], (1, 2, 0))
    wv_hld = jnp.transpose(wkv3[:, :, MLA_NOPE:], (1, 0, 2))
    bdk = (wk_hdl[:, :, None, :] * eye[:, None, :, None]).reshape(
        MLA_HEADS * MLA_NOPE, MLA_HEADS * KV_LORA).astype(BF16)
    bdv = (wv_hld[:, :, None, :] * eye[:, None, :, None]).reshape(
        MLA_HEADS * KV_LORA, MLA_HEADS * MLA_V).astype(BF16)
    wa = w_up_a[l].astype(BF16)
    wb = w_up_b[l].astype(BF16)
    wo = w_o[l].astype(BF16)
    wqt = peer_wq[l].T.astype(BF16)
    k1 = peer_keys1[l].reshape(PEER_HEADS * N_KEYS, PEER_DQ // 2).astype(BF16)
    k2 = peer_keys2[l].reshape(PEER_HEADS * N_KEYS, PEER_DQ // 2).astype(BF16)
    u_b = peer_u[l].astype(BF16)
    vt_b = peer_v[l].T.astype(BF16)
    ctab, stab = _rope_tables(ss)
    lf = jnp.broadcast_to(ret_logit_fwd[l][:, None, None], (RET_HEADS, 1, LANE))
    lb = jnp.broadcast_to(ret_logit_bwd[l][:, None, None], (RET_HEADS, 1, LANE))

    cc = jnp.concatenate([c_ctx[None, :], c, jnp.zeros((MOD_ROWS - 1 - bs, D_MODEL), F32)], axis=0)
    mod = _mod_call(cc, w_mod[l], b_mod[l][None, :])
    mod3 = jnp.concatenate([mod.reshape(MOD_ROWS, 6, D_MODEL), jnp.zeros((MOD_ROWS, 2, D_MODEL), F32)], axis=1)

    xp = x_prompt.reshape(n_p, D_MODEL)
    xs = x_sample.reshape(n_s, D_MODEL)
    qt, kk, ckvt, ckv32, kr32, rq, rk, rv, srg, sga, sgb = _proj_call(
        xp, xs, mod3, norm_mix[l][None, :], win_p, q_norm[l][None, :], kv_norm[l][None, :], wuq_all.T, bdk.T,
        ctab, stab, np_blocks, seq_blocks)

    cache_kk = jnp.concatenate([cache_ckv[:, l], _pad_cols(
        cache_krope[:, l].reshape(bs * past, MLA_ROPE), LANE).reshape(bs, past, LANE)], axis=2).astype(BF16)
    cache_vt = jnp.transpose(cache_ckv[:, l], (0, 2, 1)).astype(BF16)
    attn_p = _attn_call(qt, kk, ckvt, None, None, bdv, 0, bp, sp, sp)
    attn_s = _attn_call(qt, kk, ckvt, cache_kk, cache_vt, bdv, n_p, bs, ss, 512)

    gn = ret_gn[l][None, :]
    ret_p, sf, sb = _ret_call(rq, rk, rv, srg, gn, lf, lb, None, None, 0, bp, sp)
    ret_s, _, _ = _ret_call(rq, rk, rv, srg, gn, lf, lb, state_ret_fwd[:, l], state_ret_bwd[:, l], n_p, bs, ss)

    x1, h2 = _merge_call(attn_p, attn_s, ret_p, ret_s, sga, sgb, xp, xs, mod3, norm_ffn[l][None, :],
                          wa, wb, wo, np_blocks, seq_blocks)

    nf = norm_final[None, :]
    y_prompt = _peer_call(h2, x1, mod3, nf, wqt, k1, k2, u_b, vt_b, 0, n_p, None).reshape(bp, sp, D_MODEL)
    y_sample = _peer_call(h2, x1, mod3, nf, wqt, k1, k2, u_b, vt_b, n_p, n_s, ss).reshape(bs, ss, D_MODEL)
    new_ckv = ckv32[:n_p].reshape(bp, 1, sp, KV_LORA)
    new_kr = kr32[:n_p, :MLA_ROPE].reshape(bp, 1, sp, MLA_ROPE)
    return (y_prompt, y_sample, new_ckv, new_kr, sf[:, None], sb[:, None])
```

```python
import functools

import jax
import jax.numpy as jnp
from jax import lax
from jax.experimental import pallas as pl
from jax.experimental.pallas import tpu as pltpu

F32 = jnp.float32
BF16 = jnp.bfloat16

D_MODEL = 1024
GRID_W = 64
EPS = 1e-6
MLA_HEADS = 8
MLA_NOPE = 64
MLA_ROPE = 32
MLA_V = 64
Q_LORA = 256
KV_LORA = 128
ROPE_BASE = 10000.0
RET_HEADS = 4
RET_DK = 128
RET_DV = 128
PEER_HEADS = 8
PEER_DQ = 256
N_KEYS = 128
PEER_TOPK = 16

LANE = 128
SUBLANES = 8
LOG2_E = 1.4426950408889634
VMEM_LIMIT = 56 * 1024 * 1024

TM = 256
TQ = 128
RET_CHUNK = 256
RET_HEADS_PER_STEP = 4
TM_PEER = 512
E_CHUNK = 2048
PEER_KEYS_PER_DOT = 2
MOD_ROWS = 16

_C_CQ = 0
_C_CKV = 256
_C_KR = 384
_C_KRS = 512
_C_RQ = 640
_C_RK = 1152
_C_RV = 1664
_C_RG = 2176
_C_GA = 2688
_C_GB = 3712
_D_IN_P = 4736


def _rms(x, g):
    return x * lax.rsqrt(jnp.mean(x * x, axis=-1, keepdims=True) + EPS) * g


def _dot(a, b):
    return jnp.dot(a, b, preferred_element_type=F32)


def _dot_nt(a, b):
    return lax.dot_general(a, b, (((1,), (1,)), ((), ())), preferred_element_type=F32)


def _mod_kernel(c_ref, w_ref, b_ref, o_ref):
    c = c_ref[...]
    s = c * jax.nn.sigmoid(c)
    o_ref[...] = _dot(s.astype(BF16), w_ref[...].astype(BF16)) + b_ref[...]


def _mod_call(cc, w_mod, b_mod):
    n = w_mod.shape[1]
    bn = 1024
    return pl.pallas_call(
        _mod_kernel,
        grid=(n // bn,),
        in_specs=[
            pl.BlockSpec((MOD_ROWS, D_MODEL), lambda j: (0, 0)),
            pl.BlockSpec((D_MODEL, bn), lambda j: (0, j)),
            pl.BlockSpec((1, bn), lambda j: (0, j)),
        ],
        out_specs=pl.BlockSpec((MOD_ROWS, bn), lambda j: (0, j)),
        out_shape=jax.ShapeDtypeStruct((MOD_ROWS, n), F32),
        compiler_params=pltpu.CompilerParams(dimension_semantics=("parallel",), vmem_limit_bytes=VMEM_LIMIT),
        name="mod",
    )(cc, w_mod, b_mod)


def _proj_kernel(np_blocks, xp_ref, xs_ref, mod_ref, g_ref, win_ref, qn_ref, kvn_ref, wuqt_ref, bdkt_ref,
                 ct_ref, st_ref, ctt_ref, stt_ref,
                 qt_ref, kk_ref, ckvt_ref, ckv_ref, kr_ref, rq_ref, rk_ref, rv_ref, srg_ref, sga_ref, sgb_ref):
    x = jnp.where(pl.program_id(0) < np_blocks, xp_ref[...], xs_ref[...])
    h = _rms(x, g_ref[...]) * (1.0 + mod_ref[0, 1:2, :]) + mod_ref[0, 0:1, :]
    y = _dot(h.astype(BF16), win_ref[...])

    ct = ct_ref[...]
    st = st_ref[...]
    ckv = _rms(y[:, _C_CKV:_C_CKV + KV_LORA], kvn_ref[...])
    kr = y[:, _C_KR:_C_KR + LANE]
    krot = kr * ct + y[:, _C_KRS:_C_KRS + LANE] * st
    ckv_ref[...] = ckv
    kr_ref[...] = kr
    kk_ref[:, 0:KV_LORA] = ckv.astype(BF16)
    kk_ref[:, KV_LORA:2 * KV_LORA] = krot.astype(BF16)
    ckvt_ref[...] = ckv.T.astype(BF16)

    cq = _rms(y[:, _C_CQ:_C_CQ + Q_LORA], qn_ref[...])
    qat = _dot(wuqt_ref[...], cq.T.astype(BF16))
    nn = MLA_HEADS * MLA_NOPE
    nr = MLA_HEADS * LANE
    qlat = _dot(bdkt_ref[...], qat[:nn, :].astype(BF16))
    ctt = ctt_ref[...]
    stt = stt_ref[...]
    scale = (MLA_NOPE + MLA_ROPE) ** -0.5 * LOG2_E
    for hd in range(MLA_HEADS):
        lo = hd * LANE
        qrot = qat[nn + lo:nn + lo + LANE, :] * ctt + qat[nn + nr + lo:nn + nr + lo + LANE, :] * stt
        qt_ref[2 * lo:2 * lo + LANE, :] = (qlat[lo:lo + LANE, :] * scale).astype(BF16)
        qt_ref[2 * lo + LANE:2 * lo + 2 * LANE, :] = (qrot * scale).astype(BF16)

    hk = RET_HEADS * RET_DK
    rq_ref[...] = y[:, _C_RQ:_C_RQ + hk].astype(BF16)
    rk_ref[...] = (y[:, _C_RK:_C_RK + hk] * (RET_DK ** -0.5)).astype(BF16)
    rv_ref[...] = y[:, _C_RV:_C_RV + hk].astype(BF16)
    rg = y[:, _C_RG:_C_RG + hk]
    srg_ref[...] = (rg * jax.nn.sigmoid(rg)).astype(BF16)
    sga_ref[...] = jax.nn.sigmoid(y[:, _C_GA:_C_GA + D_MODEL]).astype(BF16)
    sgb_ref[...] = jax.nn.sigmoid(y[:, _C_GB:_C_GB + D_MODEL]).astype(BF16)


def _proj_call(xp, xs, mod3, norm_mix, win_p, q_norm, kv_norm, wuq_t, bdk_t, ctab, stab, np_blocks, seq_blocks):
    n = xp.shape[0] + xs.shape[0]
    nblk = n // TM
    rope_id_block = ctab.shape[0] // TM - 1

    def mod_idx(i):
        return (jnp.where(i < np_blocks, 0, 1 + (i - np_blocks) // seq_blocks), 0, 0)

    def rope_blk(i):
        return jnp.where(i < np_blocks, rope_id_block, (i - np_blocks) % seq_blocks)

    rope_idx = lambda i: (rope_blk(i), 0)
    rope_idx_t = lambda i: (0, rope_blk(i))
    const = lambda i: (0, 0)
    row = lambda i: (i, 0)
    col = lambda i: (0, i)
    hk = RET_HEADS * RET_DK
    out_shape = [
        jax.ShapeDtypeStruct((2 * MLA_HEADS * LANE, n), BF16),
        jax.ShapeDtypeStruct((n, 2 * LANE), BF16),
        jax.ShapeDtypeStruct((KV_LORA, n), BF16),
        jax.ShapeDtypeStruct((n, KV_LORA), F32),
        jax.ShapeDtypeStruct((n, LANE), F32),
        jax.ShapeDtypeStruct((n, hk), BF16),
        jax.ShapeDtypeStruct((n, hk), BF16),
        jax.ShapeDtypeStruct((n, hk), BF16),
        jax.ShapeDtypeStruct((n, hk), BF16),
        jax.ShapeDtypeStruct((n, D_MODEL), BF16),
        jax.ShapeDtypeStruct((n, D_MODEL), BF16),
    ]
    out_specs = [pl.BlockSpec((s.shape[0], TM), col) if s.shape[1] == n else pl.BlockSpec((TM, s.shape[1]), row)
                 for s in out_shape]
    return pl.pallas_call(
        functools.partial(_proj_kernel, np_blocks),
        grid=(nblk,),
        in_specs=[
            pl.BlockSpec((TM, D_MODEL), lambda i: (jnp.minimum(i, np_blocks - 1), 0)),
            pl.BlockSpec((TM, D_MODEL), lambda i: (jnp.maximum(i - np_blocks, 0), 0)),
            pl.BlockSpec((1, 8, D_MODEL), mod_idx),
            pl.BlockSpec((1, D_MODEL), const),
            pl.BlockSpec(win_p.shape, const),
            pl.BlockSpec((1, Q_LORA), const),
            pl.BlockSpec((1, KV_LORA), const),
            pl.BlockSpec(wuq_t.shape, const),
            pl.BlockSpec(bdk_t.shape, const),
            pl.BlockSpec((TM, LANE), rope_idx),
            pl.BlockSpec((TM, LANE), rope_idx),
            pl.BlockSpec((LANE, TM), rope_idx_t),
            pl.BlockSpec((LANE, TM), rope_idx_t),
        ],
        out_specs=out_specs,
        out_shape=out_shape,
        compiler_params=pltpu.CompilerParams(dimension_semantics=("parallel",), vmem_limit_bytes=VMEM_LIMIT),
        name="proj",
    )(xp, xs, mod3, norm_mix, win_p, q_norm, kv_norm, wuq_t, bdk_t, ctab, stab, ctab.T, stab.T)


def _attn_kernel(seq, past, tkb, *refs):
    if past:
        qt_ref, k_ref, vt_ref, ck_ref, cvt_ref, bdv_ref, o_ref, s_ref = refs
    else:
        qt_ref, k_ref, vt_ref, bdv_ref, o_ref, s_ref = refs
    blocks = []
    for kb in range(seq // tkb):
        rows = slice(kb * tkb, (kb + 1) * tkb)
        blocks.append((rows, k_ref[rows, :], vt_ref[:, rows]))
    if past:
        blocks.append((slice(seq, seq + past), ck_ref[0], cvt_ref[0]))
    fw = 2 * LANE
    qt = jnp.concatenate([qt_ref[hd * fw:(hd + 1) * fw, :] for hd in range(MLA_HEADS)], axis=1)
    m = None
    for rows, keys, _ in blocks:
        s = _dot(keys, qt)
        s_ref[rows, :] = s
        mk = jnp.max(s, axis=0, keepdims=True)
        m = mk if m is None else jnp.maximum(m, mk)
    l = None
    acc = None
    for rows, _, lat_t in blocks:
        p = jnp.exp2(s_ref[rows, :] - m)
        lk = jnp.sum(p, axis=0, keepdims=True)
        ak = _dot(lat_t, p.astype(BF16))
        l = lk if l is None else l + lk
        acc = ak if acc is None else acc + ak
    o = acc / l
    ocat = jnp.concatenate([o[:, hd * TQ:(hd + 1) * TQ].T for hd in range(MLA_HEADS)], axis=1)
    o_ref[...] = _dot(ocat.astype(BF16), bdv_ref[...]).astype(BF16)


def _attn_call(qt, kk, ckvt, cache_kk, cache_vt, bdv, tok_off, batch, seq, tkb):
    nq = seq // TQ
    past = 0 if cache_kk is None else cache_kk.shape[1]
    off = tok_off // TQ
    off_seq = tok_off // seq
    in_specs = [
        pl.BlockSpec((qt.shape[0], TQ), lambda b, qi: (0, off + b * nq + qi)),
        pl.BlockSpec((seq, 2 * LANE), lambda b, qi: (off_seq + b, 0)),
        pl.BlockSpec((KV_LORA, seq), lambda b, qi: (0, off_seq + b)),
    ]
    args = [qt, kk, ckvt]
    if past:
        in_specs += [pl.BlockSpec((1, past, 2 * LANE), lambda b, qi: (b, 0, 0)),
                     pl.BlockSpec((1, KV_LORA, past), lambda b, qi: (b, 0, 0))]
        args += [cache_kk, cache_vt]
    in_specs.append(pl.BlockSpec(bdv.shape, lambda b, qi: (0, 0)))
    args.append(bdv)
    return pl.pallas_call(
        functools.partial(_attn_kernel, seq, past, tkb),
        grid=(batch, nq),
        in_specs=in_specs,
        out_specs=pl.BlockSpec((TQ, MLA_HEADS * MLA_V), lambda b, qi: (b * nq + qi, 0)),
        out_shape=jax.ShapeDtypeStruct((batch * seq, MLA_HEADS * MLA_V), BF16),
        scratch_shapes=[pltpu.VMEM((seq + past, MLA_HEADS * TQ), F32)],
        compiler_params=pltpu.CompilerParams(
            dimension_semantics=("parallel", "parallel"), vmem_limit_bytes=VMEM_LIMIT),
        name="attn_b%d" % batch,
    )(*args)


def _log_sigmoid(x):
    return jnp.minimum(x, 0.0) - jnp.log(1.0 + jnp.exp(-jnp.abs(x)))


def _ret_kernel(has_state, seq, *refs):
    if has_state:
        (q_ref, k_ref, v_ref, srg_ref, gn_ref, lf_ref, lb_ref, s0f_ref, s0b_ref,
         o_ref, sf_ref, sb_ref, yf_ref, yb_ref) = refs
    else:
        (q_ref, k_ref, v_ref, srg_ref, gn_ref, lf_ref, lb_ref,
         o_ref, sf_ref, sb_ref, yf_ref, yb_ref) = refs
    c = RET_CHUNK
    n = seq // c
    ii = lax.broadcasted_iota(jnp.int32, (c, c), 0)
    jj = lax.broadcasted_iota(jnp.int32, (c, c), 1)
    diff = (ii - jj).astype(F32)
    idx = lax.broadcasted_iota(jnp.int32, (c, 1), 0).astype(F32)

    consts = []
    for hh in range(RET_HEADS_PER_STEP):
        lgf = _log_sigmoid(lf_ref[hh, :, 0:1])
        lgb = _log_sigmoid(lb_ref[hh, :, 0:1])
        fwd = (jnp.where(diff >= 0, jnp.exp(lgf * jnp.maximum(diff, 0.0)), 0.0),
               jnp.exp(lgf * (idx + 1.0)), jnp.exp(lgf * (c - 1.0 - idx)), jnp.exp(lgf * c))
        bwd = (jnp.where(diff <= 0, jnp.exp(lgb * jnp.maximum(-diff, 0.0)), 0.0),
               jnp.exp(lgb * (c - idx)), jnp.exp(lgb * idx), jnp.exp(lgb * c))
        consts.append((fwd, bwd))

    def chunk(hh, start, state, dec, y_ref):
        dmat, qdec, kdec, cdec = dec
        lanes = slice(hh * RET_DK, (hh + 1) * RET_DK)
        q = q_ref[pl.ds(start, c), lanes]
        k = k_ref[pl.ds(start, c), lanes]
        v = v_ref[pl.ds(start, c), lanes]
        sc = _dot_nt(q, k) * dmat
        y = _dot(sc.astype(BF16), v) + _dot(q, state.astype(BF16)) * qdec
        y_ref[pl.ds(start, c), lanes] = y
        kdt = (k.astype(F32) * kdec).T.astype(BF16)
        return state * cdec + _dot(kdt, v)

    def body(t, carry):
        out = []
        for hh in range(RET_HEADS_PER_STEP):
            sf, sb = carry[hh]
            sf = chunk(hh, pl.multiple_of(t * c, c), sf, consts[hh][0], yf_ref)
            sb = chunk(hh, pl.multiple_of((n - 1 - t) * c, c), sb, consts[hh][1], yb_ref)
            out.append((sf, sb))
        return tuple(out)

    if has_state:
        init = tuple((s0f_ref[0, hh], s0b_ref[0, hh]) for hh in range(RET_HEADS_PER_STEP))
    else:
        zero = jnp.zeros((RET_DK, RET_DV), F32)
        init = tuple((zero, zero) for _ in range(RET_HEADS_PER_STEP))
    final = lax.fori_loop(0, n, body, init)
    for hh in range(RET_HEADS_PER_STEP):
        sf_ref[0, hh] = final[hh][0]
        sb_ref[0, hh] = final[hh][1]

    for hh in range(RET_HEADS_PER_STEP):
        lanes = slice(hh * RET_DV, (hh + 1) * RET_DV)
        y = yf_ref[:, lanes] + yb_ref[:, lanes]
        mu = jnp.mean(y, axis=-1, keepdims=True)
        yc = y - mu
        var = jnp.mean(yc * yc, axis=-1, keepdims=True)
        yn = yc * lax.rsqrt(var + EPS)
        o_ref[:, lanes] = (srg_ref[:, lanes] * (yn * gn_ref[:, lanes])).astype(BF16)


def _ret_call(rq, rk, rv, srg, gn, lf, lb, s0f, s0b, tok_off, batch, seq):
    off = tok_off // seq
    has_state = s0f is not None
    hp = RET_HEADS_PER_STEP
    tokblk = lambda b, h: (off + b, h)
    st = lambda b, h: (b, h, 0, 0)
    in_specs = [
        pl.BlockSpec((seq, hp * RET_DK), tokblk),
        pl.BlockSpec((seq, hp * RET_DK), tokblk),
        pl.BlockSpec((seq, hp * RET_DV), tokblk),
        pl.BlockSpec((seq, hp * RET_DV), tokblk),
        pl.BlockSpec((1, hp * RET_DV), lambda b, h: (0, h)),
        pl.BlockSpec((hp, 1, LANE), lambda b, h: (h, 0, 0)),
        pl.BlockSpec((hp, 1, LANE), lambda b, h: (h, 0, 0)),
    ]
    args = [rq, rk, rv, srg, gn, lf, lb]
    if has_state:
        in_specs += [pl.BlockSpec((1, hp, RET_DK, RET_DV), st)] * 2
        args += [s0f, s0b]
    st_shape = jax.ShapeDtypeStruct((batch, RET_HEADS, RET_DK, RET_DV), F32)
    return pl.pallas_call(
        functools.partial(_ret_kernel, has_state, seq),
        grid=(batch, RET_HEADS // hp),
        in_specs=in_specs,
        out_specs=[
            pl.BlockSpec((seq, hp * RET_DV), lambda b, h: (b, h)),
            pl.BlockSpec((1, hp, RET_DK, RET_DV), st),
            pl.BlockSpec((1, hp, RET_DK, RET_DV), st),
        ],
        out_shape=[jax.ShapeDtypeStruct((batch * seq, RET_HEADS * RET_DV), BF16), st_shape, st_shape],
        scratch_shapes=[pltpu.VMEM((seq, hp * RET_DV), F32), pltpu.VMEM((seq, hp * RET_DV), F32)],
        compiler_params=pltpu.CompilerParams(
            dimension_semantics=("parallel", "parallel"), vmem_limit_bytes=VMEM_LIMIT),
        name="ret_b%d" % batch,
    )(*args)


def _merge_kernel(np_blocks, ap_ref, as_ref, rp_ref, rs_ref, sga_ref, sgb_ref, xp_ref, xs_ref, mod_ref,
                  wa_ref, wb_ref, wo_ref, x1_ref):
    is_prompt = pl.program_id(0) < np_blocks
    attn = jnp.where(is_prompt, ap_ref[...], as_ref[...])
    ret = jnp.where(is_prompt, rp_ref[...], rs_ref[...])
    x = jnp.where(is_prompt, xp_ref[...], xs_ref[...])
    m = sga_ref[...] * _dot(attn, wa_ref[...]) + sgb_ref[...] * _dot(ret, wb_ref[...])
    o = _dot(m.astype(BF16), wo_ref[...])
    x1 = x + mod_ref[0, 2:3, :] * o
    x1_ref[...] = x1


def _merge_call(attn_p, attn_s, ret_p, ret_s, sga, sgb, xp, xs, mod3, wa, wb, wo, np_blocks, seq_blocks):
    n = xp.shape[0] + xs.shape[0]
    nblk = n // TM
    const = lambda i: (0, 0)
    row = lambda i: (i, 0)
    p_idx = lambda i: (jnp.minimum(i, np_blocks - 1), 0)
    s_idx = lambda i: (jnp.maximum(i - np_blocks, 0), 0)

    def mod_idx(i):
        return (jnp.where(i < np_blocks, 0, 1 + (i - np_blocks) // seq_blocks), 0, 0)

    w = attn_p.shape[1]
    return pl.pallas_call(
        functools.partial(_merge_kernel, np_blocks),
        grid=(nblk,),
        in_specs=[
            pl.BlockSpec((TM, w), p_idx),
            pl.BlockSpec((TM, w), s_idx),
            pl.BlockSpec((TM, w), p_idx),
            pl.BlockSpec((TM, w), s_idx),
            pl.BlockSpec((TM, D_MODEL), row),
            pl.BlockSpec((TM, D_MODEL), row),
            pl.BlockSpec((TM, D_MODEL), p_idx),
            pl.BlockSpec((TM, D_MODEL), s_idx),
            pl.BlockSpec((1, 8, D_MODEL), mod_idx),
            pl.BlockSpec(wa.shape, const),
            pl.BlockSpec(wb.shape, const),
            pl.BlockSpec(wo.shape, const),
        ],
        out_specs=pl.BlockSpec((TM, D_MODEL), row),
        out_shape=jax.ShapeDtypeStruct((n, D_MODEL), F32),
        compiler_params=pltpu.CompilerParams(dimension_semantics=("parallel",), vmem_limit_bytes=VMEM_LIMIT),
        name="merge",
    )(attn_p, attn_s, ret_p, ret_s, sga, sgb, xp, xs, mod3, wa, wb, wo)


def _top_rows(s, k):
    rows = []
    for _ in range(k):
        m = jnp.max(s, axis=0, keepdims=True)
        rows.append(m)
        s = jnp.where(s == m, -jnp.inf, s)
    return rows


def _sort_pairs(n):
    pairs = []
    p = 1
    while p < n:
        k = p
        while k >= 1:
            for j in range(k % p, n - k, 2 * k):
                for i in range(min(k, n - j - k)):
                    if (i + j) // (2 * p) == (i + j + k) // (2 * p):
                        pairs.append((i + j, i + j + k))
            k //= 2
        p *= 2
    return pairs


def _exchange(vs, i, j):
    vs[i], vs[j] = jnp.maximum(vs[i], vs[j]), jnp.minimum(vs[i], vs[j])


def _top_sorted(blocks):
    k = len(blocks)
    vs = list(blocks)
    for i, j in _sort_pairs(k):
        _exchange(vs, i, j)
    shift = SUBLANES // 2
    while shift >= 1:
        other = [pltpu.roll(v, shift, 0) for v in vs]
        vs = [jnp.maximum(vs[i], other[k - 1 - i]) for i in range(k)]
        d = k // 2
        while d >= 1:
            for i in range(k):
                if i & d == 0:
                    _exchange(vs, i, i + d)
            d //= 2
        shift //= 2
    return vs


def _stack_rows(rows):
    n, cols = len(rows), rows[0].shape[1]
    ridx = lax.broadcasted_iota(jnp.int32, (n, cols), 0)
    out = jnp.broadcast_to(rows[0], (n, cols))
    for a in range(1, n):
        out = jnp.where(ridx == a, rows[a], out)
    return out


def _route_tile(s1, s2):
    nb = N_KEYS // SUBLANES
    assert nb == PEER_TOPK
    b1 = [s1[SUBLANES * i:SUBLANES * (i + 1), :] for i in range(nb)]
    b2 = [s2[SUBLANES * i:SUBLANES * (i + 1), :] for i in range(nb)]
    t1 = _top_sorted(b1)
    t2 = _top_sorted(b2)
    r1 = [t[0:1, :] for t in t1]
    r2 = [t[0:1, :] for t in t2]
    v1 = _stack_rows(r1)
    v2 = _stack_rows(r2)
    k8 = PEER_TOPK // 2
    cands = [r1[0] + v2, r1[1] + v2[0:k8, :], r1[2] + v2[0:k8, :], r1[3] + v2[0:k8, :],
             v1 + r2[0], v1[0:k8, :] + r2[1], v1[0:k8, :] + r2[2]]
    tau = _top_rows(jnp.concatenate(cands, axis=0), PEER_TOPK)[-1]
    e1 = [jnp.exp(r - r1[0]) for r in r1]
    e2v = jnp.exp(v2 - r2[0])
    cnts, zs = [], []
    for a in range(PEER_TOPK):
        sel = (r1[a] + v2) >= tau
        cnts.append(jnp.sum(jnp.where(sel, 1.0, 0.0), axis=0, keepdims=True))
        zs.append(jnp.sum(jnp.where(sel, e2v, 0.0), axis=0, keepdims=True))
    z = e1[0] * zs[0]
    for a in range(1, PEER_TOPK):
        z = z + e1[a] * zs[a]
    zinv = 1.0 / z
    cnts = [jnp.broadcast_to(c, b1[0].shape) for c in cnts]
    rank2, cnt = [], []
    for i in range(nb):
        rk = jnp.zeros(b2[i].shape, F32)
        ct = jnp.zeros(b1[i].shape, F32)
        for a in range(PEER_TOPK):
            rk = jnp.where(t2[a] > b2[i], float(a + 1), rk)
            ct = jnp.where(b1[i] == t1[a], cnts[a], ct)
        rank2.append(rk)
        cnt.append(ct)
    rank2 = jnp.concatenate(rank2, axis=0)
    cnt = jnp.concatenate(cnt, axis=0)
    e2 = jnp.exp(s2 - r2[0])
    cc = jnp.exp(s1 - r1[0]) * zinv
    return rank2, e2, cnt, cc


def _peer_route(qt_ref, k1_ref, k2_ref, r2_ref, e2_ref, cnt_ref, cc_ref):
    half = PEER_DQ // 2
    tm = qt_ref.shape[1]

    def head(hd, carry):
        q0 = pl.multiple_of(hd * PEER_DQ, PEER_DQ)
        k0 = pl.multiple_of(hd * N_KEYS, N_KEYS)
        k1 = k1_ref[pl.ds(k0, N_KEYS), :]
        k2 = k2_ref[pl.ds(k0, N_KEYS), :]
        for st in range(tm // LANE):
            lanes = slice(st * LANE, (st + 1) * LANE)
            s1 = _dot(k1, qt_ref[pl.ds(q0, half), lanes].astype(BF16))
            s2 = _dot(k2, qt_ref[pl.ds(q0 + half, half), lanes].astype(BF16))
            rank2, e2, cnt, cc = _route_tile(s1, s2)
            r2_ref[pl.ds(k0, N_KEYS), lanes] = rank2.astype(BF16)
            e2_ref[pl.ds(k0, N_KEYS), lanes] = e2.astype(BF16)
            cnt_ref[pl.ds(k0, N_KEYS), lanes] = cnt
            cc_ref[pl.ds(k0, N_KEYS), lanes] = cc
        return carry

    lax.fori_loop(0, PEER_HEADS, head, 0)


def _peer_kernel(gf_ref, x1_ref, mod_ref, g_ref, wqt_ref, k1_ref, k2_ref, u_ref, vt_ref,
                 y_ref, qt_ref, r2_ref, e2_ref, cnt_ref, cc_ref, acc_ref, h2_ref):
    j = pl.program_id(1)
    nj = pl.num_programs(1)
    tm = x1_ref.shape[0]

    @pl.when(j == 0)
    def _():
        h2 = _rms(x1_ref[...], gf_ref[...]) * (1.0 + mod_ref[0, 4:5, :]) + mod_ref[0, 3:4, :]
        h2_ref[...] = h2.astype(BF16)
        qt_ref[...] = _dot_nt(wqt_ref[...], h2_ref[...])
        _peer_route(qt_ref, k1_ref, k2_ref, r2_ref, e2_ref, cnt_ref, cc_ref)
        acc_ref[...] = jnp.zeros(acc_ref.shape, F32)

    parts = []
    for il in range(E_CHUNK // N_KEYS):
        if il % PEER_KEYS_PER_DOT == 0:
            rows = slice(il * N_KEYS, (il + PEER_KEYS_PER_DOT) * N_KEYS)
            hu = _dot_nt(h2_ref[...], u_ref[rows, :])
            act = jax.nn.gelu(hu.T.astype(BF16), approximate=True)
        g = None
        for hd in range(PEER_HEADS):
            r = hd * N_KEYS + j * (E_CHUNK // N_KEYS) + il
            cnt = jnp.broadcast_to(cnt_ref[pl.ds(r, 1), :], (N_KEYS, tm)).astype(BF16)
            cc = jnp.broadcast_to(cc_ref[pl.ds(r, 1), :], (N_KEYS, tm)).astype(BF16)
            kr = slice(hd * N_KEYS, (hd + 1) * N_KEYS)
            e2 = e2_ref[kr, :]
            t = jnp.where(r2_ref[kr, :] < cnt, e2, jnp.zeros_like(e2)) * cc
            g = t if g is None else g + t
        ia = il % PEER_KEYS_PER_DOT
        parts.append(g * act[ia * N_KEYS:(ia + 1) * N_KEYS, :])
    p = jnp.concatenate(parts, axis=0)
    acc_ref[...] += _dot(vt_ref[...], p)

    @pl.when(j == nj - 1)
    def _():
        x2 = x1_ref[...] + mod_ref[0, 5:6, :] * acc_ref[...].T
        y_ref[...] = _rms(x2, g_ref[...])


def _peer_call(norm_ffn, x1, mod3, norm_final, wqt, k1, k2, u_b, vt_b, tok_off, n, seq):
    tm = TM_PEER
    nblk = n // tm
    off = tok_off // tm
    ne = u_b.shape[0] // E_CHUNK
    nq = PEER_HEADS * PEER_DQ
    nk = PEER_HEADS * N_KEYS

    def mod_idx(i, j):
        return (0 if seq is None else 1 + i // (seq // tm), 0, 0)

    const = lambda i, j: (0, 0)
    tok = lambda i, j: (off + i, 0)
    return pl.pallas_call(
        _peer_kernel,
        grid=(nblk, ne),
        in_specs=[
            pl.BlockSpec((1, D_MODEL), const),
            pl.BlockSpec((tm, D_MODEL), tok),
            pl.BlockSpec((1, 8, D_MODEL), mod_idx),
            pl.BlockSpec((1, D_MODEL), const),
            pl.BlockSpec(wqt.shape, const),
            pl.BlockSpec(k1.shape, const),
            pl.BlockSpec(k2.shape, const),
            pl.BlockSpec((E_CHUNK, D_MODEL), lambda i, j: (j, 0)),
            pl.BlockSpec((D_MODEL, E_CHUNK), lambda i, j: (0, j)),
        ],
        out_specs=pl.BlockSpec((tm, D_MODEL), lambda i, j: (i, 0)),
        out_shape=jax.ShapeDtypeStruct((n, D_MODEL), F32),
        scratch_shapes=[
            pltpu.VMEM((nq, tm), F32),
            pltpu.VMEM((nk, tm), BF16),
            pltpu.VMEM((nk, tm), BF16),
            pltpu.VMEM((nk, tm), F32),
            pltpu.VMEM((nk, tm), F32),
            pltpu.VMEM((D_MODEL, tm), F32),
            pltpu.VMEM((tm, D_MODEL), BF16),
        ],
        compiler_params=pltpu.CompilerParams(
            dimension_semantics=("parallel", "arbitrary"), vmem_limit_bytes=VMEM_LIMIT),
        name="peer_n%d" % n,
    )(norm_ffn, x1, mod3, norm_final, wqt, k1, k2, u_b, vt_b)


def _rope_tables(seq):
    rows = seq // GRID_W
    row = jnp.repeat(jnp.arange(rows, dtype=F32), GRID_W)
    col = jnp.tile(jnp.arange(GRID_W, dtype=F32), rows)
    nf = MLA_ROPE // 4
    freqs = jnp.power(ROPE_BASE, -jnp.arange(nf, dtype=F32) / nf)
    ang = jnp.concatenate([row[:, None] * freqs, col[:, None] * freqs], axis=-1)
    cos, sin = jnp.cos(ang), jnp.sin(ang)
    pad = jnp.zeros((seq, LANE - MLA_ROPE), F32)
    ctab = jnp.concatenate([cos, cos, pad], axis=1)
    stab = jnp.concatenate([-sin, sin, pad], axis=1)
    cid = jnp.concatenate([jnp.ones((TM, MLA_ROPE), F32), jnp.zeros((TM, LANE - MLA_ROPE), F32)], axis=1)
    ctab = jnp.concatenate([ctab, cid], axis=0)
    stab = jnp.concatenate([stab, jnp.zeros((TM, LANE), F32)], axis=0)
    return ctab, stab


def _pad_cols(a, width):
    return jnp.concatenate([a, jnp.zeros((a.shape[0], width - a.shape[1]), a.dtype)], axis=1)


def kernel(x_prompt, x_sample, c, cache_ckv, cache_krope, state_ret_fwd, state_ret_bwd, c_ctx, w_mod, b_mod, norm_mix, norm_ffn, norm_final, w_in, q_norm, kv_norm, w_uq, w_ukv, ret_logit_fwd, ret_logit_bwd, ret_gn, w_up_a, w_up_b, w_o, peer_wq, peer_keys1, peer_keys2, peer_u, peer_v):
    depth = w_mod.shape[0]
    assert depth == 1
    bp, sp, _ = x_prompt.shape
    bs, ss, _ = x_sample.shape
    past = cache_ckv.shape[2]
    n_p, n_s = bp * sp, bs * ss
    assert 1 + bs <= MOD_ROWS and sp == TM and ss % TM == 0 and ss % TM_PEER == 0 and n_p % TM_PEER == 0
    np_blocks, seq_blocks = n_p // TM, ss // TM
    l = 0

    half = MLA_ROPE // 2
    kr1 = w_in[l][:, 384:384 + half]
    kr2 = w_in[l][:, 384 + half:384 + MLA_ROPE]
    win_p = jnp.concatenate([
        w_in[l][:, :384],
        _pad_cols(jnp.concatenate([kr1, kr2], axis=1), LANE),
        _pad_cols(jnp.concatenate([kr2, kr1], axis=1), LANE),
        w_in[l][:, 384 + MLA_ROPE:],
    ], axis=1).astype(BF16)
    assert win_p.shape[1] == _D_IN_P
    w3 = w_uq[l].reshape(Q_LORA, MLA_HEADS, MLA_NOPE + MLA_ROPE)
    r1 = w3[:, :, MLA_NOPE:MLA_NOPE + half]
    r2 = w3[:, :, MLA_NOPE + half:]
    zpad = jnp.zeros((Q_LORA, MLA_HEADS, LANE - MLA_ROPE), F32)
    wuq_all = jnp.concatenate([
        w3[:, :, :MLA_NOPE].reshape(Q_LORA, MLA_HEADS * MLA_NOPE),
        jnp.concatenate([r1, r2, zpad], axis=2).reshape(Q_LORA, MLA_HEADS * LANE),
        jnp.concatenate([r2, r1, zpad], axis=2).reshape(Q_LORA, MLA_HEADS * LANE),
    ], axis=1).astype(BF16)
    wkv3 = w_ukv[l].reshape(KV_LORA, MLA_HEADS, MLA_NOPE + MLA_V)
    eye = jnp.eye(MLA_HEADS, dtype=F32)
    wk_hdl = jnp.transpose(wkv3[:, :, :MLA_NOPE], (1, 2, 0))
    wv_hld = jnp.transpose(wkv3[:, :, MLA_NOPE:], (1, 0, 2))
    bdk = (wk_hdl[:, :, None, :] * eye[:, None, :, None]).reshape(
        MLA_HEADS * MLA_NOPE, MLA_HEADS * KV_LORA).astype(BF16)
    bdv = (wv_hld[:, :, None, :] * eye[:, None, :, None]).reshape(
        MLA_HEADS * KV_LORA, MLA_HEADS * MLA_V).astype(BF16)
    wa = w_up_a[l].astype(BF16)
    wb = w_up_b[l].astype(BF16)
    wo = w_o[l].astype(BF16)
    wqt = peer_wq[l].T.astype(BF16)
    k1 = peer_keys1[l].reshape(PEER_HEADS * N_KEYS, PEER_DQ // 2).astype(BF16)
    k2 = peer_keys2[l].reshape(PEER_HEADS * N_KEYS, PEER_DQ // 2).astype(BF16)
    u_b = peer_u[l].astype(BF16)
    vt_b = peer_v[l].T.astype(BF16)
    ctab, stab = _rope_tables(ss)
    lf = jnp.broadcast_to(ret_logit_fwd[l][:, None, None], (RET_HEADS, 1, LANE))
    lb = jnp.broadcast_to(ret_logit_bwd[l][:, None, None], (RET_HEADS, 1, LANE))

    cc = jnp.concatenate([c_ctx[None, :], c, jnp.zeros((MOD_ROWS - 1 - bs, D_MODEL), F32)], axis=0)
    mod = _mod_call(cc, w_mod[l], b_mod[l][None, :])
    mod3 = jnp.concatenate([mod.reshape(MOD_ROWS, 6, D_MODEL), jnp.zeros((MOD_ROWS, 2, D_MODEL), F32)], axis=1)

    xp = x_prompt.reshape(n_p, D_MODEL)
    xs = x_sample.reshape(n_s, D_MODEL)
    qt, kk, ckvt, ckv32, kr32, rq, rk, rv, srg, sga, sgb = _proj_call(
        xp, xs, mod3, norm_mix[l][None, :], win_p, q_norm[l][None, :], kv_norm[l][None, :], wuq_all.T, bdk.T,
        ctab, stab, np_blocks, seq_blocks)

    cache_kk = jnp.concatenate([cache_ckv[:, l], _pad_cols(
        cache_krope[:, l].reshape(bs * past, MLA_ROPE), LANE).reshape(bs, past, LANE)], axis=2).astype(BF16)
    cache_vt = jnp.transpose(cache_ckv[:, l], (0, 2, 1)).astype(BF16)
    attn_p = _attn_call(qt, kk, ckvt, None, None, bdv, 0, bp, sp, sp)
    attn_s = _attn_call(qt, kk, ckvt, cache_kk, cache_vt, bdv, n_p, bs, ss, 512)

    gn = ret_gn[l][None, :]
    ret_p, sf, sb = _ret_call(rq, rk, rv, srg, gn, lf, lb, None, None, 0, bp, sp)
    ret_s, _, _ = _ret_call(rq, rk, rv, srg, gn, lf, lb, state_ret_fwd[:, l], state_ret_bwd[:, l], n_p, bs, ss)

    x1 = _merge_call(attn_p, attn_s, ret_p, ret_s, sga, sgb, xp, xs, mod3, wa, wb, wo, np_blocks, seq_blocks)

    nf = norm_final[None, :]
    ng = norm_ffn[l][None, :]
    y_prompt = _peer_call(ng, x1, mod3, nf, wqt, k1, k2, u_b, vt_b, 0, n_p, None).reshape(bp, sp, D_MODEL)
    y_sample = _peer_call(ng, x1, mod3, nf, wqt, k1, k2, u_b, vt_b, n_p, n_s, ss).reshape(bs, ss, D_MODEL)
    new_ckv = ckv32[:n_p].reshape(bp, 1, sp, KV_LORA)
    new_kr = kr32[:n_p, :MLA_ROPE].reshape(bp, 1, sp, MLA_ROPE)
    return (y_prompt, y_sample, new_ckv, new_kr, sf[:, None], sb[:, None])
```
